```python
import math
import jax, jax.numpy as jnp
from jax import lax
import numpy as np

D_MODEL = 1024
BATCH = 8
SEQ = 2048
DEPTH = 2
DEC_BATCH = 128
DEC_SEQ = 4
PAST_LEN = 16384
PAGE_SIZE = 128

HEAD_DIM = 64
RET_HEADS = 4
RET_DK = 64
RET_DV = 64
RET_WIDTH = RET_HEADS * RET_DV
RET_CHUNK = 128
ROPE_THETA = 10000.0
SSD_HEADS = 8
SSD_HEADDIM = 64
D_INNER = SSD_HEADS * SSD_HEADDIM
SSD_GROUPS = 2
SSD_DSTATE = 128
CONV_WIDTH = 4
CONV_DIM = D_INNER + 2 * SSD_GROUPS * SSD_DSTATE
SSD_CHUNK = 128
SWA_HEADS = 4
SWA_KV_HEADS = 2
SWA_WIDTH = SWA_HEADS * HEAD_DIM
SWA_KV_WIDTH = SWA_KV_HEADS * HEAD_DIM
WINDOW = 128
SWA_BLOCK = 128
D_MIX = RET_WIDTH + D_INNER + SWA_WIDTH
IN_SIZES = (RET_HEADS * RET_DK, RET_HEADS * RET_DK, RET_WIDTH, RET_WIDTH,
            D_INNER, CONV_DIM, SSD_HEADS,
            SWA_WIDTH, SWA_KV_WIDTH, SWA_KV_WIDTH)
N_IN = sum(IN_SIZES)
N_EXPERTS = 16
N_EXPERT_GROUPS = 4
EXPERTS_PER_GROUP = N_EXPERTS // N_EXPERT_GROUPS
TOP_K = 2
D_EXPERT = 256
NORM_EPS = 1e-6

kernel_name = "hymba_style_ret_ssd_swa_moe_step"

F32 = jnp.float32


def _rmsnorm(x, g):
    xf = x.astype(F32)
    y = xf * lax.rsqrt(jnp.mean(xf * xf, axis=-1, keepdims=True) + NORM_EPS)
    return (y * g.astype(F32)).astype(x.dtype)


def _rotary(x, pos):
    half = x.shape[-1] // 2
    inv = ROPE_THETA ** (-jnp.arange(half, dtype=F32) / half)
    ang = pos.astype(F32)[:, None] * inv[None, :]
    cos = jnp.cos(ang)[:, None, :]
    sin = jnp.sin(ang)[:, None, :]
    xf = x.astype(F32)
    x1, x2 = xf[..., :half], xf[..., half:]
    return jnp.concatenate([x1 * cos - x2 * sin, x1 * sin + x2 * cos], axis=-1).astype(x.dtype)


def _retention(q, k, v, s0):
    b, T, H, dk = q.shape
    L = min(RET_CHUNK, T)
    n = T // L
    log_gamma = jnp.log(1.0 - 2.0 ** (-5.0 - jnp.arange(H, dtype=F32)))
    idx = jnp.arange(L, dtype=F32)
    diff = idx[:, None] - idx[None, :]
    intra_decay = jnp.where(diff >= 0, jnp.exp(jnp.maximum(diff, 0.0)[None] * log_gamma[:, None, None]), 0.0)
    q_decay = jnp.exp((idx + 1.0)[:, None] * log_gamma[None, :])
    k_decay = jnp.exp((L - 1.0 - idx)[:, None] * log_gamma[None, :])
    chunk_decay = jnp.exp(L * log_gamma)

    def to_chunks(t):
        return jnp.moveaxis(t.astype(F32).reshape(b, n, L, H, t.shape[-1]), 1, 0)

    qc, kc, vc = to_chunks(q), to_chunks(k * dk ** -0.5), to_chunks(v)

    def step(s, inp):
        qi, ki, vi = inp
        att = jnp.einsum('bihd,bjhd->bhij', qi, ki) * intra_decay[None]
        o = (jnp.einsum('bhij,bjhe->bihe', att, vi)
             + jnp.einsum('bihd,bhde->bihe', qi, s) * q_decay[None, :, :, None])
        s = (chunk_decay[None, :, None, None] * s
             + jnp.einsum('bjhd,bjhe->bhde', ki * k_decay[None, :, :, None], vi))
        return s, o

    s_final, o = lax.scan(step, s0.astype(F32), (qc, kc, vc))
    return jnp.moveaxis(o, 0, 1).reshape(b, T, H, -1), s_final


def _ssd(x, dt, A, bm, cm, h0):
    b, T, H, P = x.shape
    G = bm.shape[2]
    L = min(SSD_CHUNK, T)
    n = T // L
    bm_h = jnp.repeat(bm, H // G, axis=2)
    cm_h = jnp.repeat(cm, H // G, axis=2)
    tri = jnp.tril(jnp.ones((L, L), dtype=bool))

    def to_chunks(t):
        return jnp.moveaxis(t.astype(F32).reshape((b, n, L) + t.shape[2:]), 1, 0)

    def step(h, inp):
        xi, dti, bi, ci = inp
        acum = jnp.cumsum(dti * A, axis=1)
        at = jnp.swapaxes(acum, 1, 2)
        seg = jnp.exp(jnp.where(tri, at[..., :, None] - at[..., None, :], -jnp.inf))
        att = jnp.einsum('bihn,bjhn->bhij', ci, bi) * seg
        y = (jnp.einsum('bhij,bjhp->bihp', att, xi * dti[..., None])
             + jnp.einsum('bihn,bhpn->bihp', ci, h) * jnp.exp(acum)[..., None])
        w_end = jnp.exp(acum[:, -1:, :] - acum) * dti
        h = (jnp.exp(acum[:, -1])[:, :, None, None] * h
             + jnp.einsum('bjh,bjhn,bjhp->bhpn', w_end, bi, xi))
        return h, y

    h_final, y = lax.scan(step, h0.astype(F32), (to_chunks(x), to_chunks(dt), to_chunks(bm_h), to_chunks(cm_h)))
    return jnp.moveaxis(y, 0, 1).reshape(b, T, H, P), h_final


def _sink_attention(q, k, v, qpos, kpos, sinks):
    b, n, Q, H, d = q.shape
    G = k.shape[3]
    R = H // G
    qg = q.reshape(b, n, Q, G, R, d).astype(F32)
    s = jnp.einsum('bnqgrd,bnkgd->bngrqk', qg, k.astype(F32)) * d ** -0.5
    rel = qpos[:, :, None] - kpos[:, None, :]
    valid = (rel >= 0) & (rel < WINDOW) & (kpos >= 0)[:, None, :]
    s = jnp.where(valid[None, :, None, None], s, -1e30)
    sink = jnp.broadcast_to(sinks.astype(F32).reshape(G, R)[None, None, :, :, None, None], s.shape[:-1] + (1,))
    p = jax.nn.softmax(jnp.concatenate([s, sink], axis=-1), axis=-1)[..., :-1]
    o = jnp.einsum('bngrqk,bnkgd->bnqgrd', p, v.astype(F32))
    return o.reshape(b, n, Q, H * d).astype(q.dtype)


def _mixer(h, pos, ret_s0, ssd_h0, conv_buf, kv_buf, w_in, conv_w, conv_b, dt_bias, A_log, D_skip,
           ssd_norm_g, sinks, w_out):
    b, T, _ = h.shape
    proj = jnp.einsum('btd,dc->btc', h, w_in)
    rq, rk, rv, rg, z, xbc, dt, sq, sk, sv = jnp.split(proj, np.cumsum(IN_SIZES)[:-1].tolist(), axis=-1)

    rq = _rotary(rq.reshape(b, T, RET_HEADS, RET_DK), pos)
    rk = _rotary(rk.reshape(b, T, RET_HEADS, RET_DK), pos)
    ro, ret_s = _retention(rq, rk, rv.reshape(b, T, RET_HEADS, RET_DV), ret_s0)
    ro = (ro - jnp.mean(ro, -1, keepdims=True)) * lax.rsqrt(jnp.var(ro, -1, keepdims=True) + NORM_EPS)
    ro = (jax.nn.silu(rg.astype(F32)) * ro.reshape(b, T, RET_WIDTH)).astype(h.dtype)

    xp = jnp.concatenate([conv_buf.astype(xbc.dtype), xbc], axis=1)
    new_conv = xp[:, xp.shape[1] - (CONV_WIDTH - 1):]
    xc = lax.conv_general_dilated(xp, conv_w[:, None, :].astype(xp.dtype), (1,), 'VALID',
                                  dimension_numbers=('NWC', 'WIO', 'NWC'), feature_group_count=CONV_DIM)
    xc = jax.nn.silu(xc + conv_b.astype(xc.dtype))
    xs, bs, cs = jnp.split(xc, [D_INNER, D_INNER + SSD_GROUPS * SSD_DSTATE], axis=-1)
    xs = xs.reshape(b, T, SSD_HEADS, SSD_HEADDIM)
    dtv = jax.nn.softplus(dt.astype(F32) + dt_bias.astype(F32))
    A = -jnp.exp(A_log.astype(F32))
    ys, ssd_h = _ssd(xs, dtv, A, bs.reshape(b, T, SSD_GROUPS, SSD_DSTATE),
                     cs.reshape(b, T, SSD_GROUPS, SSD_DSTATE), ssd_h0)
    ys = ys + D_skip.astype(F32)[:, None] * xs.astype(F32)
    ys = _rmsnorm(ys.reshape(b, T, D_INNER) * jax.nn.silu(z.astype(F32)), ssd_norm_g).astype(h.dtype)

    sq = _rotary(sq.reshape(b, T, SWA_HEADS, HEAD_DIM), pos)
    sk = _rotary(sk.reshape(b, T, SWA_KV_HEADS, HEAD_DIM), pos)
    sv = sv.reshape(b, T, SWA_KV_HEADS, HEAD_DIM)
    if kv_buf is None:
        n = T // SWA_BLOCK
        kb = sk.reshape(b, n, SWA_BLOCK, SWA_KV_HEADS, HEAD_DIM)
        vb = sv.reshape(b, n, SWA_BLOCK, SWA_KV_HEADS, HEAD_DIM)

        def band(t):
            prev = jnp.concatenate([jnp.zeros_like(t[:, :1]), t[:, :-1]], axis=1)
            return jnp.concatenate([prev, t], axis=2)

        qpos = pos.reshape(n, SWA_BLOCK)
        kpos = qpos[:, :1] - SWA_BLOCK + jnp.arange(2 * SWA_BLOCK)[None, :]
        so = _sink_attention(sq.reshape(b, n, SWA_BLOCK, SWA_HEADS, HEAD_DIM), band(kb), band(vb), qpos, kpos, sinks)
        keep = min(WINDOW, T)
        new_k, new_v = sk[:, T - keep:], sv[:, T - keep:]
    else:
        k_buf, v_buf = kv_buf
        W = k_buf.shape[1]
        keys = jnp.concatenate([k_buf.astype(sk.dtype), sk], axis=1)
        vals = jnp.concatenate([v_buf.astype(sv.dtype), sv], axis=1)
        kpos = (pos[0] - W + jnp.arange(W + T))[None, :]
        so = _sink_attention(sq[:, None], keys[:, None], vals[:, None], pos[None, :], kpos, sinks)
        new_k, new_v = keys[:, T:], vals[:, T:]
    so = so.reshape(b, T, SWA_WIDTH)

    out = jnp.einsum('btc,cd->btd', jnp.concatenate([ro, ys, so], axis=-1), w_out)
    return out, ret_s, ssd_h, new_conv, new_k, new_v


def _moe(h, router_w, router_bias, w_gate, w_up, w_down):
    b, T, _ = h.shape
    scores = jax.nn.sigmoid(jnp.einsum('btd,de->bte', h.astype(F32), router_w.astype(F32)))
    biased = (scores + router_bias.astype(F32)).reshape(b, T, N_EXPERT_GROUPS, EXPERTS_PER_GROUP)
    group_score = jnp.sum(lax.top_k(biased, TOP_K)[0], axis=-1)
    best = jnp.argmax(group_score, axis=-1)
    in_group = best[..., None] == jnp.arange(N_EXPERT_GROUPS)
    cand = jnp.where(in_group[..., None], biased, -jnp.inf).reshape(b, T, N_EXPERTS)
    _, idx = lax.top_k(cand, TOP_K)
    w = jnp.take_along_axis(scores, idx, axis=-1)
    w = w / jnp.sum(w, axis=-1, keepdims=True)
    gates = jnp.sum((idx[..., None] == jnp.arange(N_EXPERTS)).astype(F32) * w[..., None], axis=-2)
    g = jnp.einsum('btd,edf->btef', h, w_gate)
    u = jnp.einsum('btd,edf->btef', h, w_up)
    a = jax.nn.silu(g) * u * gates[..., None].astype(h.dtype)
    return jnp.einsum('btef,efd->btd', a, w_down)


def _trunk(x, c, pos, past, params):
    (ada_w, ada_b, norm1_g, norm2_g, w_in, conv_w, conv_b, dt_bias, A_log, D_skip, ssd_norm_g, sinks,
     w_out, router_w, router_bias, w_gate, w_up, w_down) = params
    b = x.shape[0]
    new = ([], [], [], [], [])
    for l in range(DEPTH):
        if past is None:
            ret_s0 = jnp.zeros((b, RET_HEADS, RET_DK, RET_DV), F32)
            ssd_h0 = jnp.zeros((b, SSD_HEADS, SSD_HEADDIM, SSD_DSTATE), F32)
            conv_buf = jnp.zeros((b, CONV_WIDTH - 1, CONV_DIM), x.dtype)
            kv_buf = None
        else:
            ret_s0, ssd_h0, conv_buf = past[0][l], past[1][l], past[2][l]
            kv_buf = (past[3][l], past[4][l])
        mod = jnp.einsum('bd,de->be', jax.nn.silu(c), ada_w[l]) + ada_b[l]
        sh1, sc1, g1, sh2, sc2, g2 = jnp.split(mod[:, None, :], 6, axis=-1)
        h = _rmsnorm(x, norm1_g[l]) * (1.0 + sc1) + sh1
        m, ret_s, ssd_h, new_conv, new_k, new_v = _mixer(
            h, pos, ret_s0, ssd_h0, conv_buf, kv_buf, w_in[l], conv_w[l], conv_b[l], dt_bias[l], A_log[l],
            D_skip[l], ssd_norm_g[l], sinks[l], w_out[l])
        x = x + g1 * m
        h = _rmsnorm(x, norm2_g[l]) * (1.0 + sc2) + sh2
        x = x + g2 * _moe(h, router_w, router_bias, w_gate[l], w_up[l], w_down[l])
        for lst, s in zip(new, (ret_s, ssd_h, new_conv, new_k, new_v)):
            lst.append(s)
    return x, [jnp.stack(lst) for lst in new]


def setup_inputs(seed: int = 0) -> dict:
    key = jax.random.key(seed)
    keys = iter(jax.random.split(key, 40))

    def nrm(shape, scale):
        return jax.random.normal(next(keys), shape, F32) * scale

    D = D_MODEL
    w_cache = min(WINDOW, PAST_LEN)
    dt0 = jnp.exp(jax.random.uniform(next(keys), (DEPTH, SSD_HEADS), F32, math.log(1e-3), math.log(1e-1)))
    dt_bias = dt0 + jnp.log(-jnp.expm1(-dt0))
    A_log = jnp.log(jax.random.uniform(next(keys), (DEPTH, SSD_HEADS), F32, 1.0, 16.0))
    return {
        "x_prompt": nrm((BATCH, SEQ, D), 1.0),
        "x_sample": nrm((DEC_BATCH, DEC_SEQ, D), 1.0),
        "c_prompt": nrm((BATCH, D), 1.0),
        "c_sample": nrm((DEC_BATCH, D), 1.0),
        "state_ret": nrm((DEPTH, DEC_BATCH, RET_HEADS, RET_DK, RET_DV), 1.0),
        "state_ssd": nrm((DEPTH, DEC_BATCH, SSD_HEADS, SSD_HEADDIM, SSD_DSTATE), 0.1),
        "state_conv": nrm((DEPTH, DEC_BATCH, CONV_WIDTH - 1, CONV_DIM), 1.0),
        "cache_swa_k": nrm((DEPTH, DEC_BATCH, w_cache, SWA_KV_HEADS, HEAD_DIM), 1.0),
        "cache_swa_v": nrm((DEPTH, DEC_BATCH, w_cache, SWA_KV_HEADS, HEAD_DIM), 1.0),
        "ada_w": nrm((DEPTH, D, 6 * D), 0.5 * D ** -0.5),
        "ada_b": nrm((DEPTH, 6 * D), 0.02),
        "norm1_g": 1.0 + nrm((DEPTH, D), 0.02),
        "norm2_g": 1.0 + nrm((DEPTH, D), 0.02),
        "w_in": nrm((DEPTH, D, N_IN), D ** -0.5),
        "conv_w": nrm((DEPTH, CONV_WIDTH, CONV_DIM), CONV_WIDTH ** -0.5),
        "conv_b": nrm((DEPTH, CONV_DIM), 0.02),
        "dt_bias": dt_bias,
        "A_log": A_log,
        "D_skip": 1.0 + nrm((DEPTH, SSD_HEADS), 0.02),
        "ssd_norm_g": 1.0 + nrm((DEPTH, D_INNER), 0.02),
        "sinks": nrm((DEPTH, SWA_HEADS), 1.0),
        "w_out": nrm((DEPTH, D_MIX, D), D_MIX ** -0.5),
        "router_w": nrm((D, N_EXPERTS), D ** -0.5),
        "router_bias": nrm((N_EXPERTS,), 0.01),
        "w_gate": nrm((DEPTH, N_EXPERTS, D, D_EXPERT), D ** -0.5),
        "w_up": nrm((DEPTH, N_EXPERTS, D, D_EXPERT), D ** -0.5),
        "w_down": nrm((DEPTH, N_EXPERTS, D_EXPERT, D), D_EXPERT ** -0.5),
        "final_g": 1.0 + nrm((D,), 0.02),
    }


def reference(x_prompt, x_sample, c_prompt, c_sample, state_ret, state_ssd, state_conv, cache_swa_k,
              cache_swa_v, ada_w, ada_b, norm1_g, norm2_g, w_in, conv_w, conv_b, dt_bias, A_log, D_skip,
              ssd_norm_g, sinks, w_out, router_w, router_bias, w_gate, w_up, w_down, final_g):
    params = (ada_w, ada_b, norm1_g, norm2_g, w_in, conv_w, conv_b, dt_bias, A_log, D_skip, ssd_norm_g,
              sinks, w_out, router_w, router_bias, w_gate, w_up, w_down)
    T = x_prompt.shape[1]
    S = x_sample.shape[1]
    pos_p = jnp.arange(T, dtype=jnp.int32)
    pos_s = PAST_LEN + jnp.arange(S, dtype=jnp.int32)
    y_p, (ret_p, ssd_p, conv_p, k_p, v_p) = _trunk(x_prompt, c_prompt, pos_p, None, params)
    y_s, (ret_s, ssd_s, conv_s, k_s, v_s) = _trunk(
        x_sample, c_sample, pos_s, (state_ret, state_ssd, state_conv, cache_swa_k, cache_swa_v), params)
    y_prompt = _rmsnorm(y_p, final_g)
    y_sample = _rmsnorm(y_s, final_g)
    return (y_prompt, y_sample, ret_p, ssd_p, conv_p, k_p, v_p, ret_s, ssd_s, conv_s, k_s, v_s)
```

```python
import functools
import math

import jax
import jax.numpy as jnp
from jax import lax
from jax.experimental import pallas as pl
from jax.experimental.pallas import tpu as pltpu

F32 = jnp.float32
BF16 = jnp.bfloat16
HIGHEST = lax.Precision.HIGHEST

D_MODEL = 1024
DEPTH = 2
PAST_LEN = 16384
HEAD_DIM = 64
RET_HEADS = 4
RET_DK = 64
RET_DV = 64
RET_WIDTH = RET_HEADS * RET_DV
CHUNK = 128
ROPE_THETA = 10000.0
SSD_HEADS = 8
SSD_HEADDIM = 64
D_INNER = SSD_HEADS * SSD_HEADDIM
SSD_GROUPS = 2
SSD_DSTATE = 128
CONV_WIDTH = 4
CONV_DIM = D_INNER + 2 * SSD_GROUPS * SSD_DSTATE
SWA_HEADS = 4
SWA_KV_HEADS = 2
SWA_WIDTH = SWA_HEADS * HEAD_DIM
SWA_KV_WIDTH = SWA_KV_HEADS * HEAD_DIM
WINDOW = 128
N_EXPERTS = 16
N_EXPERT_GROUPS = 4
EXPERTS_PER_GROUP = N_EXPERTS // N_EXPERT_GROUPS
D_EXPERT = 256
NORM_EPS = 1e-6
LANES = 128
GATE_STRIDE = LANES // EXPERTS_PER_GROUP
VMEM_LIMIT = 56 * 1024 * 1024

_OFF_XBC = 4 * RET_WIDTH + D_INNER
_OFF_DT = _OFF_XBC + CONV_DIM
_OFF_SQ = _OFF_DT + SSD_HEADS
N_MAIN = _OFF_DT + SWA_WIDTH + 2 * SWA_KV_WIDTH


def _silu(x):
    return x * (1.0 / (1.0 + jnp.exp(-x)))


def _dot(a, b, dims, use_bf16=True):
    if use_bf16:
        a = a.astype(BF16)
        b = b.astype(BF16)
    return lax.dot_general(a, b, (dims, ((), ())), preferred_element_type=F32)


_NN = ((1,), (0,))
_NT = ((1,), (1,))
_TN = ((0,), (0,))


def _params(sem):
    return pltpu.CompilerParams(dimension_semantics=sem, vmem_limit_bytes=VMEM_LIMIT)


def _mod_kernel(c_ref, w_ref, b_ref, o_ref):
    c = c_ref[...]
    o_ref[...] = jnp.dot(_silu(c), w_ref[...], precision=HIGHEST,
                         preferred_element_type=F32) + b_ref[...]


def _modulation(c_all, ada_w, ada_b):
    rows = c_all.shape[0]
    tn = 512
    return pl.pallas_call(
        _mod_kernel,
        grid=(DEPTH, 6 * D_MODEL // tn),
        in_specs=[
            pl.BlockSpec((rows, D_MODEL), lambda l, j: (0, 0)),
            pl.BlockSpec((None, D_MODEL, tn), lambda l, j: (l, 0, j)),
            pl.BlockSpec((None, 1, tn), lambda l, j: (l, 0, j)),
        ],
        out_specs=pl.BlockSpec((None, rows, tn), lambda l, j: (l, 0, j)),
        out_shape=jax.ShapeDtypeStruct((DEPTH, rows, 6 * D_MODEL), F32),
        compiler_params=_params(("arbitrary", "arbitrary")),
        name="modulation",
    )(c_all, ada_w, ada_b.reshape(DEPTH, 1, 6 * D_MODEL))


def _rotate(p, cos, sin_signed):
    n = p.shape[-1]
    lane = lax.broadcasted_iota(jnp.int32, p.shape, 1)
    first_half = (lane % HEAD_DIM) < (HEAD_DIM // 2)
    partner = jnp.where(first_half, pltpu.roll(p, n - HEAD_DIM // 2, 1),
                        pltpu.roll(p, HEAD_DIM // 2, 1))
    return p * cos + partner * sin_signed


def _inproj_kernel(x_ref, g_ref, sc_ref, sh_ref, cos_ref, sin_ref, w_ref, wdt_ref,
                   ret_ref, z_ref, xbc_ref, dt_ref, sq_ref, sk_ref, sv_ref):
    x = x_ref[...]
    xn = x * lax.rsqrt(jnp.mean(x * x, axis=-1, keepdims=True) + NORM_EPS) * g_ref[...]
    h = (xn * (1.0 + sc_ref[...]) + sh_ref[...]).astype(BF16)
    cos = cos_ref[...]
    sin = sin_ref[...]

    def proj(lo, width):
        return jnp.dot(h, w_ref[:, lo:lo + width], preferred_element_type=F32)

    w = RET_WIDTH
    ret_ref[:, 0:w] = _rotate(proj(0, w), cos, sin).astype(ret_ref.dtype)
    ret_ref[:, w:2 * w] = (_rotate(proj(w, w), cos, sin) * RET_DK ** -0.5).astype(ret_ref.dtype)
    ret_ref[:, 2 * w:4 * w] = proj(2 * w, 2 * w).astype(ret_ref.dtype)
    z_ref[...] = proj(4 * w, D_INNER).astype(z_ref.dtype)
    xbc_ref[...] = proj(_OFF_XBC, CONV_DIM)
    dt_ref[...] = jnp.dot(h, wdt_ref[...], preferred_element_type=F32)
    lo = _OFF_DT
    sq_ref[...] = (_rotate(proj(lo, SWA_WIDTH), cos, sin) * HEAD_DIM ** -0.5).astype(sq_ref.dtype)
    lo += SWA_WIDTH
    sk_ref[...] = _rotate(proj(lo, SWA_KV_WIDTH), cos[:, :SWA_KV_WIDTH], sin[:, :SWA_KV_WIDTH])
    sv_ref[...] = proj(lo + SWA_KV_WIDTH, SWA_KV_WIDTH)


def _inproj(x2d, gain, mod3, cos3, sin3, w_main, w_dt, *, tm, act_dtype):
    n = x2d.shape[0]
    nt = n // tm
    per_mod = nt // mod3.shape[0]
    npos = cos3.shape[0]
    r = mod3.shape[1]
    row = lambda i: (i, 0)
    outs = [
        (4 * RET_WIDTH, act_dtype), (D_INNER, act_dtype), (CONV_DIM, F32), (LANES, F32),
        (SWA_WIDTH, act_dtype), (SWA_KV_WIDTH, F32), (SWA_KV_WIDTH, F32),
    ]
    return pl.pallas_call(
        _inproj_kernel,
        grid=(nt,),
        in_specs=[
            pl.BlockSpec((tm, D_MODEL), row),
            pl.BlockSpec((1, D_MODEL), lambda i: (0, 0)),
            pl.BlockSpec((None, r, D_MODEL), lambda i: (i // per_mod, 0, 1)),
            pl.BlockSpec((None, r, D_MODEL), lambda i: (i // per_mod, 0, 0)),
            pl.BlockSpec((None, tm, SWA_WIDTH), lambda i: (i % npos, 0, 0)),
            pl.BlockSpec((None, tm, SWA_WIDTH), lambda i: (i % npos, 0, 0)),
            pl.BlockSpec((D_MODEL, N_MAIN), lambda i: (0, 0)),
            pl.BlockSpec((D_MODEL, LANES), lambda i: (0, 0)),
        ],
        out_specs=[pl.BlockSpec((tm, c), row) for c, _ in outs],
        out_shape=[jax.ShapeDtypeStruct((n, c), dt) for c, dt in outs],
        compiler_params=_params(("arbitrary",)),
        name="inproj",
    )(x2d, gain, mod3, mod3, cos3, sin3, w_main, w_dt)


def _ret_kernel(blk_ref, s0_ref, ro_ref, s_ref, *, use_bf16):
    L = blk_ref.shape[0]

    @pl.when(pl.program_id(1) == 0)
    def _():
        s_ref[...] = s0_ref[...]

    blk = blk_ref[...]
    row = lax.broadcasted_iota(jnp.int32, (L, L), 0)
    col = lax.broadcasted_iota(jnp.int32, (L, L), 1)
    diff = (row - col).astype(F32)
    idx = lax.broadcasted_iota(jnp.int32, (L, 1), 0).astype(F32)
    w = RET_WIDTH
    outs = []
    for h in range(RET_HEADS):
        log_gamma = math.log(1.0 - 2.0 ** (-5.0 - h))
        lo = h * RET_DK
        q = blk[:, lo:lo + RET_DK]
        k = blk[:, w + lo:w + lo + RET_DK]
        v = blk[:, 2 * w + lo:2 * w + lo + RET_DV]
        gate = blk[:, 3 * w + lo:3 * w + lo + RET_DV].astype(F32)
        decay = jnp.where(diff >= 0, jnp.exp(jnp.maximum(diff, 0.0) * log_gamma), 0.0)
        q_decay = jnp.exp((idx + 1.0) * log_gamma)
        k_decay = jnp.exp((L - 1.0 - idx) * log_gamma)
        s = s_ref[h]
        att = _dot(q, k, _NT, use_bf16) * decay
        o = _dot(att, v, _NN, use_bf16) + _dot(q, s, _NN, use_bf16) * q_decay
        s_ref[h] = math.exp(L * log_gamma) * s + _dot(k.astype(F32) * k_decay, v, _TN, use_bf16)
        mean = jnp.mean(o, axis=-1, keepdims=True)
        cen = o - mean
        var = jnp.mean(cen * cen, axis=-1, keepdims=True)
        outs.append(_silu(gate) * (cen * lax.rsqrt(var + NORM_EPS)))
    ro_ref[...] = jnp.concatenate(outs, axis=-1).astype(ro_ref.dtype)


def _retention(ret3, s0, *, nc, out_dtype, use_bf16):
    nblk, L, _ = ret3.shape
    b = nblk // nc
    state_spec = pl.BlockSpec((None, RET_HEADS, RET_DK, RET_DV), lambda i, c: (i, 0, 0, 0))
    return pl.pallas_call(
        functools.partial(_ret_kernel, use_bf16=use_bf16),
        grid=(b, nc),
        in_specs=[pl.BlockSpec((None, L, 4 * RET_WIDTH), lambda i, c: (i * nc + c, 0, 0)), state_spec],
        out_specs=[pl.BlockSpec((None, L, RET_WIDTH), lambda i, c: (i * nc + c, 0, 0)), state_spec],
        out_shape=[jax.ShapeDtypeStruct((nblk, L, RET_WIDTH), out_dtype),
                   jax.ShapeDtypeStruct(s0.shape, F32)],
        compiler_params=_params(("arbitrary", "arbitrary")),
        name="retention",
    )(ret3, s0)


def _softplus(x):
    return jnp.maximum(x, 0.0) + jnp.log1p(jnp.exp(-jnp.abs(x)))


def _ssd_kernel(xbc_ref, z_ref, dt_ref, tail0_ref, h0_ref, cw_ref, cb_ref, hp_ref, ng_ref,
                ys_ref, h_ref, tail_ref, *, use_bf16):
    L = xbc_ref.shape[0]
    keep = CONV_WIDTH - 1

    @pl.when(pl.program_id(1) == 0)
    def _():
        h_ref[...] = h0_ref[...]
        tail_ref[...] = tail0_ref[...]

    cur = xbc_ref[...]
    tail = tail_ref[...]
    ext = jnp.concatenate([tail, cur], axis=0)
    cw = cw_ref[...]
    xc = cw[keep:keep + 1] * cur
    for wi in range(keep):
        lo = 8 - keep + wi
        xc = xc + cw[wi:wi + 1] * ext[lo:lo + L]
    if L >= 8:
        tail_ref[...] = cur[L - 8:]
    xc = _silu(xc + cb_ref[...])
    xs = xc[:, :D_INNER]
    bs = xc[:, D_INNER:D_INNER + SSD_GROUPS * SSD_DSTATE]
    cs = xc[:, D_INNER + SSD_GROUPS * SSD_DSTATE:]

    hp = hp_ref[...]
    dtv = _softplus(dt_ref[...] + hp[0:1])
    a = dtv * (-jnp.exp(hp[1:2]))
    row = lax.broadcasted_iota(jnp.int32, (L, L), 0)
    col = lax.broadcasted_iota(jnp.int32, (L, L), 1)
    tri = row >= col
    acum = jnp.dot(tri.astype(F32), a, precision=HIGHEST, preferred_element_type=F32)
    pick = (lax.broadcasted_iota(jnp.int32, (8, LANES), 0)
            == lax.broadcasted_iota(jnp.int32, (8, LANES), 1)).astype(F32)
    acum_t = lax.dot_general(pick, acum, (_NT, ((), ())), precision=HIGHEST,
                             preferred_element_type=F32)
    dskip = hp[2:3]

    cb = [_dot(cs[:, g * SSD_DSTATE:(g + 1) * SSD_DSTATE], bs[:, g * SSD_DSTATE:(g + 1) * SSD_DSTATE],
               _NT, use_bf16) for g in range(SSD_GROUPS)]
    per_group = SSD_HEADS // SSD_GROUPS
    outs = []
    for h in range(SSD_HEADS):
        g = h // per_group
        bg = bs[:, g * SSD_DSTATE:(g + 1) * SSD_DSTATE]
        cg = cs[:, g * SSD_DSTATE:(g + 1) * SSD_DSTATE]
        x_h = xs[:, h * SSD_HEADDIM:(h + 1) * SSD_HEADDIM]
        a_col = acum[:, h:h + 1]
        a_row = acum_t[h:h + 1, :]
        dt_h = dtv[:, h:h + 1]
        last = acum[L - 1:L, h:h + 1]
        seg = jnp.exp(jnp.where(tri, a_col - a_row, -jnp.inf))
        state = h_ref[h]
        y = (_dot(cb[g] * seg, x_h * dt_h, _NN, use_bf16)
             + _dot(cg, state, _NT, use_bf16) * jnp.exp(a_col))
        w_end = jnp.exp(last - a_col) * dt_h
        h_ref[h] = jnp.exp(last) * state + _dot(x_h * w_end, bg, _TN, use_bf16)
        outs.append(y + dskip[:, h:h + 1] * x_h)
    ys = jnp.concatenate(outs, axis=-1) * _silu(z_ref[...].astype(F32))
    ys = ys * lax.rsqrt(jnp.mean(ys * ys, axis=-1, keepdims=True) + NORM_EPS) * ng_ref[...]
    ys_ref[...] = ys.astype(ys_ref.dtype)


def _ssd(xbc3, z3, dt3, tail0, h0, conv_w, conv_b, head_params, norm_g, *, nc, out_dtype, use_bf16):
    nblk, L, _ = xbc3.shape
    b = nblk // nc
    assert L >= 8 or nc == 1, "the conv tail is only carried between chunks of at least 8 rows"
    blk = lambda c: pl.BlockSpec((None, L, c), lambda i, j: (i * nc + j, 0, 0))
    const = lambda s: pl.BlockSpec(s, lambda i, j: (0,) * len(s))
    state_spec = pl.BlockSpec((None, SSD_HEADS, SSD_HEADDIM, SSD_DSTATE), lambda i, j: (i, 0, 0, 0))
    return pl.pallas_call(
        functools.partial(_ssd_kernel, use_bf16=use_bf16),
        grid=(b, nc),
        in_specs=[blk(CONV_DIM), blk(D_INNER), blk(LANES),
                  pl.BlockSpec((None, 8, CONV_DIM), lambda i, j: (i, 0, 0)), state_spec,
                  const((CONV_WIDTH, CONV_DIM)), const((1, CONV_DIM)), const((8, LANES)),
                  const((1, D_INNER))],
        out_specs=[blk(D_INNER), state_spec],
        out_shape=[jax.ShapeDtypeStruct((nblk, L, D_INNER), out_dtype),
                   jax.ShapeDtypeStruct(h0.shape, F32)],
        scratch_shapes=[pltpu.VMEM((8, CONV_DIM), F32)],
        compiler_params=_params(("arbitrary", "arbitrary")),
        name="ssd",
    )(xbc3, z3, dt3, tail0, h0, conv_w, conv_b, head_params, norm_g)


def _sink_softmax_pv(parts, sink, use_bf16):
    m = sink
    for s, _ in parts:
        m = jnp.maximum(m, jnp.max(s, axis=-1, keepdims=True))
    den = jnp.exp(sink - m)
    o = None
    for s, v in parts:
        p = jnp.exp(s - m)
        den = den + jnp.sum(p, axis=-1, keepdims=True)
        pv = _dot(p, v, _NN, use_bf16)
        o = pv if o is None else o + pv
    return o / den


def _swa_band_kernel(q_ref, kp_ref, kc_ref, vp_ref, vc_ref, sink_ref, o_ref):
    L = q_ref.shape[0]
    c = pl.program_id(1)
    q = q_ref[...]
    keys = jnp.concatenate([kp_ref[...], kc_ref[...]], axis=0).astype(BF16)
    vals = jnp.concatenate([vp_ref[...], vc_ref[...]], axis=0).astype(BF16)
    row = lax.broadcasted_iota(jnp.int32, (L, 2 * L), 0)
    col = lax.broadcasted_iota(jnp.int32, (L, 2 * L), 1)
    rel = row - col + L
    valid = (rel >= 0) & (rel < WINDOW) & ((col >= L) | (c > 0))
    sinks = sink_ref[...]
    rep = SWA_HEADS // SWA_KV_HEADS
    outs = []
    for h in range(SWA_HEADS):
        g = h // rep
        s = _dot(q[:, h * HEAD_DIM:(h + 1) * HEAD_DIM], keys[:, g * HEAD_DIM:(g + 1) * HEAD_DIM], _NT)
        s = jnp.where(valid, s, -1e30)
        outs.append(_sink_softmax_pv([(s, vals[:, g * HEAD_DIM:(g + 1) * HEAD_DIM])],
                                     sinks[:, h:h + 1], True))
    o_ref[...] = jnp.concatenate(outs, axis=-1).astype(o_ref.dtype)


def _swa_band(sq3, sk3, sv3, sinks_row, *, nc):
    nblk, L, _ = sq3.shape
    b = nblk // nc
    cur = lambda c: pl.BlockSpec((None, L, c), lambda i, j: (i * nc + j, 0, 0))
    prev = lambda c: pl.BlockSpec((None, L, c), lambda i, j: (i * nc + jnp.maximum(j - 1, 0), 0, 0))
    return pl.pallas_call(
        _swa_band_kernel,
        grid=(b, nc),
        in_specs=[cur(SWA_WIDTH), prev(SWA_KV_WIDTH), cur(SWA_KV_WIDTH), prev(SWA_KV_WIDTH),
                  cur(SWA_KV_WIDTH), pl.BlockSpec((1, LANES), lambda i, j: (0, 0))],
        out_specs=cur(SWA_WIDTH),
        out_shape=jax.ShapeDtypeStruct((nblk, L, SWA_WIDTH), BF16),
        compiler_params=_params(("arbitrary", "arbitrary")),
        name="swa_band",
    )(sq3, sk3, sk3, sv3, sv3, sinks_row)


def _swa_cache_kernel(q_ref, kn_ref, vn_ref, kc_ref, vc_ref, sink_ref, o_ref):
    T = q_ref.shape[0]
    W = kc_ref.shape[0]
    q = q_ref[...]
    kn = kn_ref[...]
    vn = vn_ref[...]
    kc = kc_ref[...]
    vc = vc_ref[...]
    qt = lax.broadcasted_iota(jnp.int32, (T, W), 0)
    kcol = lax.broadcasted_iota(jnp.int32, (T, W), 1)
    rel_c = qt + W - kcol
    valid_c = (rel_c >= 0) & (rel_c < WINDOW) & (kcol + (PAST_LEN - W) >= 0)
    rel_n = lax.broadcasted_iota(jnp.int32, (T, T), 0) - lax.broadcasted_iota(jnp.int32, (T, T), 1)
    valid_n = (rel_n >= 0) & (rel_n < WINDOW)
    sinks = sink_ref[...]
    rep = SWA_HEADS // SWA_KV_HEADS
    outs = []
    for h in range(SWA_HEADS):
        g = h // rep
        gs = slice(g * HEAD_DIM, (g + 1) * HEAD_DIM)
        qh = q[:, h * HEAD_DIM:(h + 1) * HEAD_DIM]
        s_c = jnp.where(valid_c, _dot(qh, kc[:, gs], _NT, False), -1e30)
        s_n = jnp.where(valid_n, _dot(qh, kn[:, gs], _NT, False), -1e30)
        outs.append(_sink_softmax_pv([(s_c, vc[:, gs]), (s_n, vn[:, gs])], sinks[:, h:h + 1], False))
    o_ref[...] = jnp.concatenate(outs, axis=-1).astype(o_ref.dtype)


def _swa_cache(sq3, sk3, sv3, k_cache, v_cache, sinks_row):
    b, T, _ = sq3.shape
    W = k_cache.shape[1]
    new = lambda c: pl.BlockSpec((None, T, c), lambda i: (i, 0, 0))
    cache = pl.BlockSpec((None, W, SWA_KV_WIDTH), lambda i: (i, 0, 0))
    return pl.pallas_call(
        _swa_cache_kernel,
        grid=(b,),
        in_specs=[new(SWA_WIDTH), new(SWA_KV_WIDTH), new(SWA_KV_WIDTH), cache, cache,
                  pl.BlockSpec((1, LANES), lambda i: (0, 0))],
        out_specs=new(SWA_WIDTH),
        out_shape=jax.ShapeDtypeStruct((b, T, SWA_WIDTH), F32),
        compiler_params=_params(("arbitrary",)),
        name="swa_cache",
    )(sq3, sk3, sv3, k_cache, v_cache, sinks_row)


def _router_gates(hn, wr_ref, rb_ref):
    logits = jnp.dot(hn, wr_ref[...], precision=HIGHEST, preferred_element_type=F32)
    scores = 1.0 / (1.0 + jnp.exp(-logits))
    biased = scores + rb_ref[...]
    lane = lax.broadcasted_iota(jnp.int32, logits.shape, 1)
    group_lane = lane < N_EXPERT_GROUPS
    sk = [scores] + [pltpu.roll(scores, LANES - GATE_STRIDE * k, 1) for k in range(1, EXPERTS_PER_GROUP)]
    bk = [biased] + [pltpu.roll(biased, LANES - GATE_STRIDE * k, 1) for k in range(1, EXPERTS_PER_GROUP)]
    hi1, lo1 = jnp.maximum(bk[0], bk[1]), jnp.minimum(bk[0], bk[1])
    hi2, lo2 = jnp.maximum(bk[2], bk[3]), jnp.minimum(bk[2], bk[3])
    top1 = jnp.maximum(hi1, hi2)
    top2 = jnp.maximum(jnp.minimum(hi1, hi2), jnp.maximum(lo1, lo2))
    group_score = jnp.where(group_lane, top1 + top2, -jnp.inf)
    best_score = jnp.max(group_score, axis=-1, keepdims=True)
    best = jnp.min(jnp.where(group_score == best_score, lane, LANES), axis=-1, keepdims=True)
    in_group = lane == best
    picked = []
    for k in range(EXPERTS_PER_GROUP):
        rank = jnp.zeros(logits.shape, jnp.int32)
        for j in range(EXPERTS_PER_GROUP):
            if j == k:
                continue
            ahead = (bk[j] >= bk[k]) if j < k else (bk[j] > bk[k])
            rank = rank + ahead.astype(jnp.int32)
        picked.append(jnp.where((rank < 2) & in_group, sk[k], 0.0))
    total = jnp.sum(picked[0] + picked[1] + picked[2] + picked[3], axis=-1, keepdims=True)
    gates = picked[0] / total
    for k in range(1, EXPERTS_PER_GROUP):
        gates = gates + pltpu.roll(picked[k] / total, GATE_STRIDE * k, 1)
    return gates


def _outproj_kernel(ro_ref, ys_ref, so_ref, x_ref, g1_ref, sc_ref, sh_ref, n2_ref, w_ref, wr_ref, rb_ref,
                    x1_ref, h2_ref, gate_ref):
    m = jnp.dot(ro_ref[...].astype(BF16), w_ref[0:RET_WIDTH, :], preferred_element_type=F32)
    m = m + jnp.dot(ys_ref[...].astype(BF16), w_ref[RET_WIDTH:RET_WIDTH + D_INNER, :],
                    preferred_element_type=F32)
    m = m + jnp.dot(so_ref[...].astype(BF16), w_ref[RET_WIDTH + D_INNER:, :], preferred_element_type=F32)
    x1 = x_ref[...] + g1_ref[...] * m
    x1_ref[...] = x1
    xn = x1 * lax.rsqrt(jnp.mean(x1 * x1, axis=-1, keepdims=True) + NORM_EPS) * n2_ref[...]
    hn = xn * (1.0 + sc_ref[...]) + sh_ref[...]
    h2_ref[...] = hn.astype(BF16)
    gate_ref[...] = _router_gates(hn, wr_ref, rb_ref)


def _outproj(ro, ys, so, x2d, mod3, gain2, w_out, w_router, router_bias, *, tm):
    n = x2d.shape[0]
    nt = n // tm
    per_mod = nt // mod3.shape[0]
    r = mod3.shape[1]
    row = lambda c: pl.BlockSpec((tm, c), lambda i: (i, 0))
    mod = lambda j: pl.BlockSpec((None, r, D_MODEL), lambda i: (i // per_mod, 0, j))
    const = lambda s: pl.BlockSpec(s, lambda i: (0, 0))
    return pl.pallas_call(
        _outproj_kernel,
        grid=(nt,),
        in_specs=[row(RET_WIDTH), row(D_INNER), row(SWA_WIDTH), row(D_MODEL),
                  mod(2), mod(4), mod(3), const((1, D_MODEL)), const((D_MODEL, D_MODEL)),
                  const((D_MODEL, LANES)), const((1, LANES))],
        out_specs=[row(D_MODEL), row(D_MODEL), row(LANES)],
        out_shape=[jax.ShapeDtypeStruct((n, D_MODEL), F32), jax.ShapeDtypeStruct((n, D_MODEL), BF16),
                   jax.ShapeDtypeStruct((n, LANES), F32)],
        compiler_params=_params(("arbitrary",)),
        name="outproj",
    )(ro, ys, so, x2d, mod3, mod3, mod3, gain2, w_out, w_router, router_bias)


def _moe_kernel(h_ref, gate_ref, x1_ref, g2_ref, wg_ref, wu_ref, wd_ref, fg_ref, o_ref, acc_ref, *,
                final_norm):
    e = pl.program_id(1)

    @pl.when(e == 0)
    def _():
        acc_ref[...] = jnp.zeros_like(acc_ref)

    h = h_ref[...]
    gates = gate_ref[...]
    lane = lax.broadcasted_iota(jnp.int32, gates.shape, 1)
    target = GATE_STRIDE * (e % EXPERTS_PER_GROUP) + e // EXPERTS_PER_GROUP
    gate = jnp.sum(jnp.where(lane == target, gates, 0.0), axis=-1, keepdims=True)
    g = jnp.dot(h, wg_ref[...], preferred_element_type=F32)
    u = jnp.dot(h, wu_ref[...], preferred_element_type=F32)
    a = (_silu(g) * u * gate).astype(BF16)
    acc_ref[...] += jnp.dot(a, wd_ref[...], preferred_element_type=F32)

    @pl.when(e == N_EXPERTS - 1)
    def _():
        x2 = x1_ref[...] + g2_ref[...] * acc_ref[...]
        if final_norm:
            x2 = x2 * lax.rsqrt(jnp.mean(x2 * x2, axis=-1, keepdims=True) + NORM_EPS) * fg_ref[...]
        o_ref[...] = x2


def _moe(h2, gates, x1, mod3, w_gate, w_up, w_down, final_g, *, tm, final_norm):
    n = h2.shape[0]
    nt = n // tm
    per_mod = nt // mod3.shape[0]
    r = mod3.shape[1]
    row = lambda c: pl.BlockSpec((tm, c), lambda i, e: (i, 0))
    return pl.pallas_call(
        functools.partial(_moe_kernel, final_norm=final_norm),
        grid=(nt, N_EXPERTS),
        in_specs=[row(D_MODEL), row(LANES), row(D_MODEL),
                  pl.BlockSpec((None, r, D_MODEL), lambda i, e: (i // per_mod, 0, 5)),
                  pl.BlockSpec((None, D_MODEL, D_EXPERT), lambda i, e: (e, 0, 0)),
                  pl.BlockSpec((None, D_MODEL, D_EXPERT), lambda i, e: (e, 0, 0)),
                  pl.BlockSpec((None, D_EXPERT, D_MODEL), lambda i, e: (e, 0, 0)),
                  pl.BlockSpec((1, D_MODEL), lambda i, e: (0, 0))],
        out_specs=row(D_MODEL),
        out_shape=jax.ShapeDtypeStruct((n, D_MODEL), F32),
        scratch_shapes=[pltpu.VMEM((tm, D_MODEL), F32)],
        compiler_params=_params(("arbitrary", "arbitrary")),
        name="moe",
    )(h2, gates, x1, mod3, w_gate, w_up, w_down, final_g)


def _rope_tables(pos):
    half = HEAD_DIM // 2
    inv = ROPE_THETA ** (-jnp.arange(half, dtype=F32) / half)
    ang = pos.astype(F32)[:, None] * inv[None, :]
    cos = jnp.cos(ang)
    sin = jnp.sin(ang)
    cos = jnp.tile(jnp.concatenate([cos, cos], axis=-1), (1, SWA_WIDTH // HEAD_DIM))
    sin = jnp.tile(jnp.concatenate([-sin, sin], axis=-1), (1, SWA_WIDTH // HEAD_DIM))
    return cos, sin


def _pad_lanes(v):
    return jnp.pad(v, (0, LANES - v.shape[0])).reshape(1, LANES)


def _layer_weights(l, w_in, dt_bias, A_log, D_skip, sinks, w_out, w_gate, w_up, w_down):
    wl = w_in[l]
    w_main = jnp.concatenate([wl[:, :_OFF_DT], wl[:, _OFF_SQ:]], axis=1).astype(BF16)
    w_dt = jnp.pad(wl[:, _OFF_DT:_OFF_SQ], ((0, 0), (0, LANES - SSD_HEADS))).astype(BF16)
    head_params = jnp.concatenate(
        [_pad_lanes(dt_bias[l]), _pad_lanes(A_log[l]), _pad_lanes(D_skip[l]), jnp.zeros((5, LANES), F32)], axis=0)
    return dict(w_main=w_main, w_dt=w_dt, head_params=head_params, sinks=_pad_lanes(sinks[l]),
                w_out=w_out[l].astype(BF16), w_gate=w_gate[l].astype(BF16), w_up=w_up[l].astype(BF16),
                w_down=w_down[l].astype(BF16))


def _router_layout(router_w, router_bias):
    e = jnp.arange(N_EXPERTS)
    lanes = GATE_STRIDE * (e % EXPERTS_PER_GROUP) + e // EXPERTS_PER_GROUP
    w = jnp.zeros((D_MODEL, LANES), F32).at[:, lanes].set(router_w.astype(F32))
    b = jnp.zeros((1, LANES), F32).at[0, lanes].set(router_bias.astype(F32))
    return w, b


def _trunk(x, mod, pos, past, layers, shared, *, tm, chunk, act_dtype, use_bf16):
    b, T, _ = x.shape
    n = b * T
    nc = T // chunk
    cos, sin = _rope_tables(pos)
    reps = tm // T if tm > T else 1
    cos3 = jnp.tile(cos, (reps, 1)).reshape(-1, tm, SWA_WIDTH)
    sin3 = jnp.tile(sin, (reps, 1)).reshape(-1, tm, SWA_WIDTH)
    x2d = x.reshape(n, D_MODEL)
    new = ([], [], [], [], [])
    for l in range(DEPTH):
        lw = layers[l]
        ret, z, xbc, dt, sq, sk, sv = _inproj(
            x2d, shared["norm1_g"][l], mod[l], cos3, sin3, lw["w_main"], lw["w_dt"], tm=tm, act_dtype=act_dtype)
        blocks = lambda a: a.reshape(b * nc, chunk, a.shape[-1])
        if past is None:
            s0 = jnp.zeros((b, RET_HEADS, RET_DK, RET_DV), F32)
            h0 = jnp.zeros((b, SSD_HEADS, SSD_HEADDIM, SSD_DSTATE), F32)
            tail0 = jnp.zeros((b, 8, CONV_DIM), F32)
        else:
            s0, h0 = past[0][l], past[1][l]
            tail0 = jnp.pad(past[2][l], ((0, 0), (8 - (CONV_WIDTH - 1), 0), (0, 0)))
        ro, ret_s = _retention(blocks(ret), s0, nc=nc, out_dtype=act_dtype, use_bf16=use_bf16)
        ys, ssd_h = _ssd(blocks(xbc), blocks(z), blocks(dt), tail0, h0, shared["conv_w"][l],
                         shared["conv_b"][l], lw["head_params"], shared["ssd_norm_g"][l],
                         nc=nc, out_dtype=act_dtype, use_bf16=use_bf16)
        sk3 = sk.reshape(b, T, SWA_KV_WIDTH)
        sv3 = sv.reshape(b, T, SWA_KV_WIDTH)
        if past is None:
            so = _swa_band(blocks(sq), blocks(sk), blocks(sv), lw["sinks"], nc=nc)
            keep = min(WINDOW, T)
            new_k, new_v = sk3[:, T - keep:], sv3[:, T - keep:]
        else:
            kc = past[3][l].reshape(b, -1, SWA_KV_WIDTH)
            vc = past[4][l].reshape(b, -1, SWA_KV_WIDTH)
            so = _swa_cache(sq.reshape(b, T, SWA_WIDTH), sk3, sv3, kc, vc, lw["sinks"])
            new_k = jnp.concatenate([kc, sk3], axis=1)[:, T:]
            new_v = jnp.concatenate([vc, sv3], axis=1)[:, T:]
        x1, h2, gates = _outproj(
            ro.reshape(n, RET_WIDTH), ys.reshape(n, D_INNER), so.reshape(n, SWA_WIDTH), x2d, mod[l],
            shared["norm2_g"][l], lw["w_out"], shared["w_router"], shared["router_bias"], tm=tm)
        x2d = _moe(h2, gates, x1, mod[l], lw["w_gate"], lw["w_up"], lw["w_down"], shared["final_g"],
                   tm=tm, final_norm=(l == DEPTH - 1))
        new_conv = xbc.reshape(b, T, CONV_DIM)[:, T - (CONV_WIDTH - 1):]
        kv_shape = (b, -1, SWA_KV_HEADS, HEAD_DIM)
        for lst, s in zip(new, (ret_s, ssd_h, new_conv, new_k.reshape(kv_shape), new_v.reshape(kv_shape))):
            lst.append(s)
    return x2d.reshape(b, T, D_MODEL), [jnp.stack(lst) for lst in new]


def kernel(x_prompt, x_sample, c_prompt, c_sample, state_ret, state_ssd, state_conv, cache_swa_k,
           cache_swa_v, ada_w, ada_b, norm1_g, norm2_g, w_in, conv_w, conv_b, dt_bias, A_log, D_skip,
           ssd_norm_g, sinks, w_out, router_w, router_bias, w_gate, w_up, w_down, final_g):
    bp, T, _ = x_prompt.shape
    bs, S, _ = x_sample.shape
    layers = [_layer_weights(l, w_in, dt_bias, A_log, D_skip, sinks, w_out, w_gate, w_up, w_down)
              for l in range(DEPTH)]
    w_router, rb = _router_layout(router_w, router_bias)
    shared = dict(norm1_g=norm1_g.reshape(DEPTH, 1, D_MODEL), norm2_g=norm2_g.reshape(DEPTH, 1, D_MODEL),
                  conv_w=conv_w, conv_b=conv_b.reshape(DEPTH, 1, CONV_DIM),
                  ssd_norm_g=ssd_norm_g.reshape(DEPTH, 1, D_INNER), w_router=w_router, router_bias=rb,
                  final_g=final_g.reshape(1, D_MODEL))
    mod = _modulation(jnp.concatenate([c_prompt, c_sample], axis=0), ada_w, ada_b)
    tm_s = bs * S
    mod_p = mod[:, :bp].reshape(DEPTH, bp, 1, 6 * D_MODEL)
    mod_s = jnp.repeat(mod[:, bp:], S, axis=1).reshape(DEPTH, 1, tm_s, 6 * D_MODEL)
    pos_p = jnp.arange(T, dtype=jnp.int32)
    pos_s = PAST_LEN + jnp.arange(S, dtype=jnp.int32)
    y_p, new_p = _trunk(x_prompt, mod_p, pos_p, None, layers, shared,
                        tm=512, chunk=CHUNK, act_dtype=BF16, use_bf16=True)
    y_s, new_s = _trunk(x_sample, mod_s, pos_s, (state_ret, state_ssd, state_conv, cache_swa_k, cache_swa_v),
                        layers, shared, tm=tm_s, chunk=S, act_dtype=F32, use_bf16=False)
    return (y_p, y_s, *new_p, *new_s)
```

```python
import functools
import math

import jax
import jax.numpy as jnp
from jax import lax
from jax.experimental import pallas as pl
from jax.experimental.pallas import tpu as pltpu

F32 = jnp.float32
BF16 = jnp.bfloat16
HIGHEST = lax.Precision.HIGHEST

D_MODEL = 1024
DEPTH = 2
PAST_LEN = 16384
HEAD_DIM = 64
RET_HEADS = 4
RET_DK = 64
RET_DV = 64
RET_WIDTH = RET_HEADS * RET_DV
CHUNK = 128
ROPE_THETA = 10000.0
SSD_HEADS = 8
SSD_HEADDIM = 64
D_INNER = SSD_HEADS * SSD_HEADDIM
SSD_GROUPS = 2
SSD_DSTATE = 128
CONV_WIDTH = 4
CONV_DIM = D_INNER + 2 * SSD_GROUPS * SSD_DSTATE
SWA_HEADS = 4
SWA_KV_HEADS = 2
SWA_WIDTH = SWA_HEADS * HEAD_DIM
SWA_KV_WIDTH = SWA_KV_HEADS * HEAD_DIM
WINDOW = 128
N_EXPERTS = 16
N_EXPERT_GROUPS = 4
EXPERTS_PER_GROUP = N_EXPERTS // N_EXPERT_GROUPS
D_EXPERT = 256
NORM_EPS = 1e-6
LANES = 128
SUBLANES = 8
GATE_STRIDE = LANES // EXPERTS_PER_GROUP
VMEM_LIMIT = 56 * 1024 * 1024
TOKEN_TILE = 512
SEQS_PER_STEP = 8

_OFF_XBC = 4 * RET_WIDTH + D_INNER
_OFF_DT = _OFF_XBC + CONV_DIM
_OFF_SQ = _OFF_DT + SSD_HEADS
N_MAIN = _OFF_DT + SWA_WIDTH + 2 * SWA_KV_WIDTH


def _silu(x):
    return x * (1.0 / (1.0 + jnp.exp(-x)))


def _dot(a, b, dims, use_bf16=True):
    if use_bf16:
        a = a.astype(BF16)
        b = b.astype(BF16)
    return lax.dot_general(a, b, (dims, ((), ())), preferred_element_type=F32)


_NN = ((1,), (0,))
_NT = ((1,), (1,))
_TN = ((0,), (0,))


def _params(sem):
    return pltpu.CompilerParams(dimension_semantics=sem, vmem_limit_bytes=VMEM_LIMIT)


class _LayerState:
    def __init__(self, shape, layer, seqs, init, prev):
        self.shape, self.init, self.prev = shape, init, prev
        tail = (0,) * (len(shape) - 2)
        self.spec = pl.BlockSpec((None, seqs) + tuple(shape[2:]), lambda i, c: (layer, i) + tail)

    def add_inputs(self, operands, in_specs, aliases, out_index):
        if self.init is not None:
            operands.append(self.init)
            in_specs.append(self.spec)
        if self.prev is not None:
            aliases[len(operands)] = out_index
            operands.append(self.prev)
            in_specs.append(pl.BlockSpec(memory_space=pl.ANY))

    @property
    def out_shape(self):
        return jax.ShapeDtypeStruct(self.shape, F32)


def _mod_kernel(c_ref, w_ref, b_ref, o_ref):
    c = c_ref[...]
    o_ref[...] = jnp.dot(_silu(c), w_ref[...], precision=HIGHEST,
                         preferred_element_type=F32) + b_ref[...]


def _modulation(c_all, ada_w, ada_b):
    rows = c_all.shape[0]
    tn = 512
    return pl.pallas_call(
        _mod_kernel,
        grid=(DEPTH, 6 * D_MODEL // tn),
        in_specs=[
            pl.BlockSpec((rows, D_MODEL), lambda l, j: (0, 0)),
            pl.BlockSpec((None, D_MODEL, tn), lambda l, j: (l, 0, j)),
            pl.BlockSpec((None, 1, tn), lambda l, j: (l, 0, j)),
        ],
        out_specs=pl.BlockSpec((None, rows, tn), lambda l, j: (l, 0, j)),
        out_shape=jax.ShapeDtypeStruct((DEPTH, rows, 6 * D_MODEL), F32),
        compiler_params=_params(("arbitrary", "arbitrary")),
        name="modulation",
    )(c_all, ada_w, ada_b.reshape(DEPTH, 1, 6 * D_MODEL))


def _rotate(p, cos, sin_signed):
    n = p.shape[-1]
    lane = lax.broadcasted_iota(jnp.int32, p.shape, 1)
    first_half = (lane % HEAD_DIM) < (HEAD_DIM // 2)
    partner = jnp.where(first_half, pltpu.roll(p, n - HEAD_DIM // 2, 1),
                        pltpu.roll(p, HEAD_DIM // 2, 1))
    return p * cos + partner * sin_signed


def _inproj_kernel(x_ref, g_ref, sc_ref, sh_ref, cos_ref, sin_ref, w_ref, ws_ref, wdt_ref,
                   ret_ref, z_ref, xbc_ref, dt_ref, sq_ref, sk_ref, sv_ref):
    x = x_ref[...]
    xn = x * lax.rsqrt(jnp.mean(x * x, axis=-1, keepdims=True) + NORM_EPS) * g_ref[...]
    h = (xn * (1.0 + sc_ref[...]) + sh_ref[...]).astype(BF16)
    cos = cos_ref[...]
    sin = sin_ref[...]

    def proj(ref, lo, width):
        return jnp.dot(h, ref[:, lo:lo + width], preferred_element_type=F32)

    w = RET_WIDTH
    ret_ref[:, 0:w] = _rotate(proj(w_ref, 0, w), cos, sin).astype(ret_ref.dtype)
    ret_ref[:, w:2 * w] = (_rotate(proj(w_ref, w, w), cos, sin) * RET_DK ** -0.5).astype(ret_ref.dtype)
    ret_ref[:, 2 * w:4 * w] = proj(w_ref, 2 * w, 2 * w).astype(ret_ref.dtype)
    z_ref[...] = proj(w_ref, 4 * w, D_INNER).astype(z_ref.dtype)
    xbc_ref[...] = proj(w_ref, _OFF_XBC, CONV_DIM)
    dt_ref[...] = jnp.dot(h, wdt_ref[...], preferred_element_type=F32)
    sq_ref[...] = (_rotate(proj(ws_ref, 0, SWA_WIDTH), cos, sin) * HEAD_DIM ** -0.5).astype(sq_ref.dtype)
    sk_ref[...] = _rotate(proj(ws_ref, SWA_WIDTH, SWA_KV_WIDTH), cos[:, :SWA_KV_WIDTH], sin[:, :SWA_KV_WIDTH])
    sv_ref[...] = proj(ws_ref, SWA_WIDTH + SWA_KV_WIDTH, SWA_KV_WIDTH)


def _inproj(x2d, gain, mod3, cos3, sin3, w_main, w_swa, w_dt, *, tm, act_dtype):
    n = x2d.shape[0]
    nt = n // tm
    per_mod = nt // mod3.shape[0]
    npos = cos3.shape[0]
    r = mod3.shape[1]
    row = lambda i: (i, 0)
    const = lambda a: pl.BlockSpec(a.shape, lambda i: (0, 0))
    outs = [
        (4 * RET_WIDTH, act_dtype), (D_INNER, act_dtype), (CONV_DIM, F32), (LANES, F32),
        (SWA_WIDTH, act_dtype), (SWA_KV_WIDTH, F32), (SWA_KV_WIDTH, F32),
    ]
    return pl.pallas_call(
        _inproj_kernel,
        grid=(nt,),
        in_specs=[
            pl.BlockSpec((tm, D_MODEL), row),
            const(gain),
            pl.BlockSpec((None, r, D_MODEL), lambda i: (i // per_mod, 0, 1)),
            pl.BlockSpec((None, r, D_MODEL), lambda i: (i // per_mod, 0, 0)),
            pl.BlockSpec((None, tm, SWA_WIDTH), lambda i: (i % npos, 0, 0)),
            pl.BlockSpec((None, tm, SWA_WIDTH), lambda i: (i % npos, 0, 0)),
            const(w_main), const(w_swa), const(w_dt),
        ],
        out_specs=[pl.BlockSpec((tm, c), row) for c, _ in outs],
        out_shape=[jax.ShapeDtypeStruct((n, c), dt) for c, dt in outs],
        compiler_params=_params(("arbitrary",)),
        name="inproj",
    )(x2d, gain, mod3, mod3, cos3, sin3, w_main, w_swa, w_dt)


def _ret_kernel(*refs, has_init, use_bf16):
    blk_ref = refs[0]
    s0_ref = refs[1] if has_init else None
    ro_ref, s_ref = refs[-2:]
    seqs, L = blk_ref.shape[0], blk_ref.shape[1]

    @pl.when(pl.program_id(1) == 0)
    def _():
        s_ref[...] = s0_ref[...] if has_init else jnp.zeros(s_ref.shape, F32)

    row = lax.broadcasted_iota(jnp.int32, (L, L), 0)
    col = lax.broadcasted_iota(jnp.int32, (L, L), 1)
    diff = (row - col).astype(F32)
    idx = lax.broadcasted_iota(jnp.int32, (L, 1), 0).astype(F32)
    w = RET_WIDTH
    for j in range(seqs):
        blk = blk_ref[j]
        outs = []
        for h in range(RET_HEADS):
            log_gamma = math.log(1.0 - 2.0 ** (-5.0 - h))
            lo = h * RET_DK
            q = blk[:, lo:lo + RET_DK]
            k = blk[:, w + lo:w + lo + RET_DK]
            v = blk[:, 2 * w + lo:2 * w + lo + RET_DV]
            gate = blk[:, 3 * w + lo:3 * w + lo + RET_DV].astype(F32)
            decay = jnp.where(diff >= 0, jnp.exp(jnp.maximum(diff, 0.0) * log_gamma), 0.0)
            q_decay = jnp.exp((idx + 1.0) * log_gamma)
            k_decay = jnp.exp((L - 1.0 - idx) * log_gamma)
            s = s_ref[j, h]
            att = _dot(q, k, _NT, use_bf16) * decay
            o = _dot(att, v, _NN, use_bf16) + _dot(q, s, _NN, use_bf16) * q_decay
            s_ref[j, h] = math.exp(L * log_gamma) * s + _dot(k.astype(F32) * k_decay, v, _TN, use_bf16)
            mean = jnp.mean(o, axis=-1, keepdims=True)
            cen = o - mean
            var = jnp.mean(cen * cen, axis=-1, keepdims=True)
            outs.append(_silu(gate) * (cen * lax.rsqrt(var + NORM_EPS)))
        ro_ref[j] = jnp.concatenate(outs, axis=-1).astype(ro_ref.dtype)


def _retention(ret3, state, *, nc, seqs, out_dtype, use_bf16):
    nblk, L, _ = ret3.shape
    b = nblk // nc
    blk = lambda c: pl.BlockSpec((seqs, L, c), lambda i, j: (i * nc + j, 0, 0))
    operands, in_specs, aliases = [ret3], [blk(4 * RET_WIDTH)], {}
    state.add_inputs(operands, in_specs, aliases, 1)
    return pl.pallas_call(
        functools.partial(_ret_kernel, has_init=state.init is not None, use_bf16=use_bf16),
        grid=(b // seqs, nc),
        in_specs=in_specs,
        out_specs=[blk(RET_WIDTH), state.spec],
        out_shape=[jax.ShapeDtypeStruct((nblk, L, RET_WIDTH), out_dtype), state.out_shape],
        input_output_aliases=aliases,
        compiler_params=_params(("arbitrary", "arbitrary")),
        name="retention",
    )(*operands)


def _softplus(x):
    return jnp.maximum(x, 0.0) + jnp.log1p(jnp.exp(-jnp.abs(x)))


def _ssd_kernel(*refs, has_init, use_bf16):
    xbc_ref, z_ref, dt_ref, cw_ref, cb_ref, hp_ref, ng_ref = refs[:7]
    tail0_ref, h0_ref = (refs[7], refs[8]) if has_init else (None, None)
    ys_ref, h_ref, tail_ref = refs[-3:]
    seqs, L = xbc_ref.shape[0], xbc_ref.shape[1]
    keep = CONV_WIDTH - 1

    @pl.when(pl.program_id(1) == 0)
    def _():
        h_ref[...] = h0_ref[...] if has_init else jnp.zeros(h_ref.shape, F32)
        tail_ref[...] = tail0_ref[...] if has_init else jnp.zeros(tail_ref.shape, F32)

    cw = cw_ref[...]
    hp = hp_ref[...]
    row = lax.broadcasted_iota(jnp.int32, (L, L), 0)
    col = lax.broadcasted_iota(jnp.int32, (L, L), 1)
    tri = row >= col
    pick = (lax.broadcasted_iota(jnp.int32, (SUBLANES, LANES), 0)
            == lax.broadcasted_iota(jnp.int32, (SUBLANES, LANES), 1)).astype(F32)
    dskip = hp[2:3]
    per_group = SSD_HEADS // SSD_GROUPS
    n = SSD_DSTATE
    for j in range(seqs):
        cur = xbc_ref[j]
        ext = jnp.concatenate([tail_ref[j], cur], axis=0)
        xc = cw[keep:keep + 1] * cur
        for wi in range(keep):
            lo = SUBLANES - keep + wi
            xc = xc + cw[wi:wi + 1] * ext[lo:lo + L]
        if L >= SUBLANES:
            tail_ref[j] = cur[L - SUBLANES:]
        xc = _silu(xc + cb_ref[...])
        xs = xc[:, :D_INNER]
        bs = xc[:, D_INNER:D_INNER + SSD_GROUPS * n]
        cs = xc[:, D_INNER + SSD_GROUPS * n:]

        dtv = _softplus(dt_ref[j] + hp[0:1])
        a = dtv * (-jnp.exp(hp[1:2]))
        acum = jnp.dot(tri.astype(F32), a, precision=HIGHEST, preferred_element_type=F32)
        acum_t = lax.dot_general(pick, acum, (_NT, ((), ())), precision=HIGHEST,
                                 preferred_element_type=F32)
        cb = [_dot(cs[:, g * n:(g + 1) * n], bs[:, g * n:(g + 1) * n], _NT, use_bf16)
              for g in range(SSD_GROUPS)]
        outs = []
        for h in range(SSD_HEADS):
            g = h // per_group
            bg = bs[:, g * n:(g + 1) * n]
            cg = cs[:, g * n:(g + 1) * n]
            x_h = xs[:, h * SSD_HEADDIM:(h + 1) * SSD_HEADDIM]
            a_col = acum[:, h:h + 1]
            a_row = acum_t[h:h + 1, :]
            dt_h = dtv[:, h:h + 1]
            last = acum[L - 1:L, h:h + 1]
            seg = jnp.exp(jnp.where(tri, a_col - a_row, -jnp.inf))
            state = h_ref[j, h]
            y = (_dot(cb[g] * seg, x_h * dt_h, _NN, use_bf16)
                 + _dot(cg, state, _NT, use_bf16) * jnp.exp(a_col))
            w_end = jnp.exp(last - a_col) * dt_h
            h_ref[j, h] = jnp.exp(last) * state + _dot(x_h * w_end, bg, _TN, use_bf16)
            outs.append(y + dskip[:, h:h + 1] * x_h)
        ys = jnp.concatenate(outs, axis=-1) * _silu(z_ref[j].astype(F32))
        ys = ys * lax.rsqrt(jnp.mean(ys * ys, axis=-1, keepdims=True) + NORM_EPS) * ng_ref[...]
        ys_ref[j] = ys.astype(ys_ref.dtype)


def _ssd(xbc3, z3, dt3, tail0, state, conv_w, conv_b, head_params, norm_g, *, nc, seqs, out_dtype, use_bf16):
    nblk, L, _ = xbc3.shape
    b = nblk // nc
    assert L >= SUBLANES or nc == 1, "the conv tail is only carried between chunks of at least 8 rows"
    blk = lambda c: pl.BlockSpec((seqs, L, c), lambda i, j: (i * nc + j, 0, 0))
    const = lambda a: pl.BlockSpec(a.shape, lambda i, j: (0,) * a.ndim)
    operands = [xbc3, z3, dt3, conv_w, conv_b, head_params, norm_g]
    in_specs = [blk(CONV_DIM), blk(D_INNER), blk(LANES), const(conv_w), const(conv_b), const(head_params),
                const(norm_g)]
    aliases = {}
    if state.init is not None:
        operands.append(tail0)
        in_specs.append(pl.BlockSpec((seqs, SUBLANES, CONV_DIM), lambda i, j: (i, 0, 0)))
    state.add_inputs(operands, in_specs, aliases, 1)
    return pl.pallas_call(
        functools.partial(_ssd_kernel, has_init=state.init is not None, use_bf16=use_bf16),
        grid=(b // seqs, nc),
        in_specs=in_specs,
        out_specs=[blk(D_INNER), state.spec],
        out_shape=[jax.ShapeDtypeStruct((nblk, L, D_INNER), out_dtype), state.out_shape],
        scratch_shapes=[pltpu.VMEM((seqs, SUBLANES, CONV_DIM), F32)],
        input_output_aliases=aliases,
        compiler_params=_params(("arbitrary", "arbitrary")),
        name="ssd",
    )(*operands)


def _sink_softmax_pv(parts, sink, use_bf16):
    m = sink
    for s, _ in parts:
        m = jnp.maximum(m, jnp.max(s, axis=-1, keepdims=True))
    den = jnp.exp(sink - m)
    o = None
    for s, v in parts:
        p = jnp.exp(s - m)
        den = den + jnp.sum(p, axis=-1, keepdims=True)
        pv = _dot(p, v, _NN, use_bf16)
        o = pv if o is None else o + pv
    return o / den


def _swa_band_kernel(q_ref, kp_ref, kc_ref, vp_ref, vc_ref, sink_ref, o_ref):
    L = q_ref.shape[0]
    c = pl.program_id(1)
    q = q_ref[...]
    keys = jnp.concatenate([kp_ref[...], kc_ref[...]], axis=0).astype(BF16)
    vals = jnp.concatenate([vp_ref[...], vc_ref[...]], axis=0).astype(BF16)
    row = lax.broadcasted_iota(jnp.int32, (L, 2 * L), 0)
    col = lax.broadcasted_iota(jnp.int32, (L, 2 * L), 1)
    rel = row - col + L
    valid = (rel >= 0) & (rel < WINDOW) & ((col >= L) | (c > 0))
    sinks = sink_ref[...]
    rep = SWA_HEADS // SWA_KV_HEADS
    outs = []
    for h in range(SWA_HEADS):
        g = h // rep
        s = _dot(q[:, h * HEAD_DIM:(h + 1) * HEAD_DIM], keys[:, g * HEAD_DIM:(g + 1) * HEAD_DIM], _NT)
        s = jnp.where(valid, s, -1e30)
        outs.append(_sink_softmax_pv([(s, vals[:, g * HEAD_DIM:(g + 1) * HEAD_DIM])],
                                     sinks[:, h:h + 1], True))
    o_ref[...] = jnp.concatenate(outs, axis=-1).astype(o_ref.dtype)


def _swa_band(sq3, sk3, sv3, sinks_row, *, nc):
    nblk, L, _ = sq3.shape
    b = nblk // nc
    cur = lambda c: pl.BlockSpec((None, L, c), lambda i, j: (i * nc + j, 0, 0))
    prev = lambda c: pl.BlockSpec((None, L, c), lambda i, j: (i * nc + jnp.maximum(j - 1, 0), 0, 0))
    return pl.pallas_call(
        _swa_band_kernel,
        grid=(b, nc),
        in_specs=[cur(SWA_WIDTH), prev(SWA_KV_WIDTH), cur(SWA_KV_WIDTH), prev(SWA_KV_WIDTH),
                  cur(SWA_KV_WIDTH), pl.BlockSpec((1, LANES), lambda i, j: (0, 0))],
        out_specs=cur(SWA_WIDTH),
        out_shape=jax.ShapeDtypeStruct((nblk, L, SWA_WIDTH), BF16),
        compiler_params=_params(("arbitrary", "arbitrary")),
        name="swa_band",
    )(sq3, sk3, sk3, sv3, sv3, sinks_row)


def _swa_cache_kernel(q_ref, kn_ref, vn_ref, kc_ref, vc_ref, sink_ref, o_ref):
    seqs, T = q_ref.shape[0], q_ref.shape[1]
    W = kc_ref.shape[1]
    qt = lax.broadcasted_iota(jnp.int32, (T, W), 0)
    kcol = lax.broadcasted_iota(jnp.int32, (T, W), 1)
    rel_c = qt + W - kcol
    valid_c = (rel_c >= 0) & (rel_c < WINDOW) & (kcol + (PAST_LEN - W) >= 0)
    rel_n = lax.broadcasted_iota(jnp.int32, (T, T), 0) - lax.broadcasted_iota(jnp.int32, (T, T), 1)
    valid_n = (rel_n >= 0) & (rel_n < WINDOW)
    sinks = sink_ref[...]
    rep = SWA_HEADS // SWA_KV_HEADS
    for j in range(seqs):
        q, kn, vn, kc, vc = q_ref[j], kn_ref[j], vn_ref[j], kc_ref[j], vc_ref[j]
        outs = []
        for h in range(SWA_HEADS):
            g = h // rep
            gs = slice(g * HEAD_DIM, (g + 1) * HEAD_DIM)
            qh = q[:, h * HEAD_DIM:(h + 1) * HEAD_DIM]
            s_c = jnp.where(valid_c, _dot(qh, kc[:, gs], _NT, False), -1e30)
            s_n = jnp.where(valid_n, _dot(qh, kn[:, gs], _NT, False), -1e30)
            outs.append(_sink_softmax_pv([(s_c, vc[:, gs]), (s_n, vn[:, gs])], sinks[:, h:h + 1], False))
        o_ref[j] = jnp.concatenate(outs, axis=-1).astype(o_ref.dtype)


def _swa_cache(sq3, sk3, sv3, k_cache, v_cache, sinks_row, *, seqs):
    b, T, _ = sq3.shape
    W = k_cache.shape[1]
    new = lambda c: pl.BlockSpec((seqs, T, c), lambda i: (i, 0, 0))
    cache = pl.BlockSpec((seqs, W, SWA_KV_WIDTH), lambda i: (i, 0, 0))
    return pl.pallas_call(
        _swa_cache_kernel,
        grid=(b // seqs,),
        in_specs=[new(SWA_WIDTH), new(SWA_KV_WIDTH), new(SWA_KV_WIDTH), cache, cache,
                  pl.BlockSpec((1, LANES), lambda i: (0, 0))],
        out_specs=new(SWA_WIDTH),
        out_shape=jax.ShapeDtypeStruct((b, T, SWA_WIDTH), F32),
        compiler_params=_params(("arbitrary",)),
        name="swa_cache",
    )(sq3, sk3, sv3, k_cache, v_cache, sinks_row)


def _router_gates(hn, h_hi, wr_hi_ref, wr_lo_ref, rb_ref):
    h_lo = (hn - h_hi.astype(F32)).astype(BF16)
    logits = (jnp.dot(h_hi, wr_hi_ref[...], preferred_element_type=F32)
              + jnp.dot(h_lo, wr_hi_ref[...], preferred_element_type=F32)
              + jnp.dot(h_hi, wr_lo_ref[...], preferred_element_type=F32))
    scores = 1.0 / (1.0 + jnp.exp(-logits))
    biased = scores + rb_ref[...]
    lane = lax.broadcasted_iota(jnp.int32, logits.shape, 1)
    group_lane = lane < N_EXPERT_GROUPS
    sk = [scores] + [pltpu.roll(scores, LANES - GATE_STRIDE * k, 1) for k in range(1, EXPERTS_PER_GROUP)]
    bk = [biased] + [pltpu.roll(biased, LANES - GATE_STRIDE * k, 1) for k in range(1, EXPERTS_PER_GROUP)]
    hi1, lo1 = jnp.maximum(bk[0], bk[1]), jnp.minimum(bk[0], bk[1])
    hi2, lo2 = jnp.maximum(bk[2], bk[3]), jnp.minimum(bk[2], bk[3])
    top1 = jnp.maximum(hi1, hi2)
    top2 = jnp.maximum(jnp.minimum(hi1, hi2), jnp.maximum(lo1, lo2))
    group_score = jnp.where(group_lane, top1 + top2, -jnp.inf)
    best_score = jnp.max(group_score, axis=-1, keepdims=True)
    best = jnp.min(jnp.where(group_score == best_score, lane, LANES), axis=-1, keepdims=True)
    in_group = lane == best
    picked = []
    for k in range(EXPERTS_PER_GROUP):
        rank = jnp.zeros(logits.shape, jnp.int32)
        for j in range(EXPERTS_PER_GROUP):
            if j == k:
                continue
            ahead = (bk[j] >= bk[k]) if j < k else (bk[j] > bk[k])
            rank = rank + ahead.astype(jnp.int32)
        picked.append(jnp.where((rank < 2) & in_group, sk[k], 0.0))
    total = jnp.sum(picked[0] + picked[1] + picked[2] + picked[3], axis=-1, keepdims=True)
    gates = picked[0] / total
    for k in range(1, EXPERTS_PER_GROUP):
        gates = gates + pltpu.roll(picked[k] / total, GATE_STRIDE * k, 1)
    return gates


def _outproj_kernel(ro_ref, ys_ref, so_ref, x_ref, g1_ref, sc_ref, sh_ref, n2_ref, w_ref, wr_hi_ref, wr_lo_ref,
                    rb_ref, x1_ref, h2_ref, gate_ref):
    m = jnp.dot(ro_ref[...].astype(BF16), w_ref[0:RET_WIDTH, :], preferred_element_type=F32)
    m = m + jnp.dot(ys_ref[...].astype(BF16), w_ref[RET_WIDTH:RET_WIDTH + D_INNER, :],
                    preferred_element_type=F32)
    m = m + jnp.dot(so_ref[...].astype(BF16), w_ref[RET_WIDTH + D_INNER:, :], preferred_element_type=F32)
    x1 = x_ref[...] + g1_ref[...] * m
    x1_ref[...] = x1
    xn = x1 * lax.rsqrt(jnp.mean(x1 * x1, axis=-1, keepdims=True) + NORM_EPS) * n2_ref[...]
    hn = xn * (1.0 + sc_ref[...]) + sh_ref[...]
    h_hi = hn.astype(BF16)
    h2_ref[...] = h_hi
    gate_ref[...] = _router_gates(hn, h_hi, wr_hi_ref, wr_lo_ref, rb_ref)


def _outproj(ro, ys, so, x2d, mod3, gain2, w_out, wr_hi, wr_lo, router_bias, *, tm):
    n = x2d.shape[0]
    nt = n // tm
    per_mod = nt // mod3.shape[0]
    r = mod3.shape[1]
    row = lambda c: pl.BlockSpec((tm, c), lambda i: (i, 0))
    mod = lambda j: pl.BlockSpec((None, r, D_MODEL), lambda i: (i // per_mod, 0, j))
    const = lambda a: pl.BlockSpec(a.shape, lambda i: (0, 0))
    return pl.pallas_call(
        _outproj_kernel,
        grid=(nt,),
        in_specs=[row(RET_WIDTH), row(D_INNER), row(SWA_WIDTH), row(D_MODEL),
                  mod(2), mod(4), mod(3), const(gain2), const(w_out), const(wr_hi), const(wr_lo),
                  const(router_bias)],
        out_specs=[row(D_MODEL), row(D_MODEL), row(LANES)],
        out_shape=[jax.ShapeDtypeStruct((n, D_MODEL), F32), jax.ShapeDtypeStruct((n, D_MODEL), BF16),
                   jax.ShapeDtypeStruct((n, LANES), F32)],
        compiler_params=_params(("arbitrary",)),
        name="outproj",
    )(ro, ys, so, x2d, mod3, mod3, mod3, gain2, w_out, wr_hi, wr_lo, router_bias)


def _moe_kernel(h_ref, gate_ref, x1_ref, g2_ref, wg_ref, wu_ref, wd_ref, fg_ref, o_ref, *, final_norm):
    h = h_ref[...]
    gates = gate_ref[...]
    acts = []
    for e in range(N_EXPERTS):
        lane = GATE_STRIDE * (e % EXPERTS_PER_GROUP) + e // EXPERTS_PER_GROUP
        g = jnp.dot(h, wg_ref[e], preferred_element_type=F32)
        u = jnp.dot(h, wu_ref[e], preferred_element_type=F32)
        acts.append((_silu(g) * u * gates[:, lane:lane + 1]).astype(BF16))
    mixed = jnp.dot(jnp.concatenate(acts, axis=-1), wd_ref[...], preferred_element_type=F32)
    x2 = x1_ref[...] + g2_ref[...] * mixed
    if final_norm:
        x2 = x2 * lax.rsqrt(jnp.mean(x2 * x2, axis=-1, keepdims=True) + NORM_EPS) * fg_ref[...]
    o_ref[...] = x2


def _moe(h2, gates, x1, mod3, w_gate, w_up, w_down, final_g, *, tm, final_norm):
    n = h2.shape[0]
    nt = n // tm
    per_mod = nt // mod3.shape[0]
    r = mod3.shape[1]
    row = lambda c: pl.BlockSpec((tm, c), lambda i: (i, 0))
    const = lambda a: pl.BlockSpec(a.shape, lambda i: (0,) * a.ndim)
    return pl.pallas_call(
        functools.partial(_moe_kernel, final_norm=final_norm),
        grid=(nt,),
        in_specs=[row(D_MODEL), row(LANES), row(D_MODEL),
                  pl.BlockSpec((None, r, D_MODEL), lambda i: (i // per_mod, 0, 5)),
                  const(w_gate), const(w_up), const(w_down), const(final_g)],
        out_specs=row(D_MODEL),
        out_shape=jax.ShapeDtypeStruct((n, D_MODEL), F32),
        compiler_params=_params(("arbitrary",)),
        name="moe",
    )(h2, gates, x1, mod3, w_gate, w_up, w_down, final_g)


def _rope_tables(pos):
    half = HEAD_DIM // 2
    inv = ROPE_THETA ** (-jnp.arange(half, dtype=F32) / half)
    ang = pos.astype(F32)[:, None] * inv[None, :]
    cos = jnp.cos(ang)
    sin = jnp.sin(ang)
    cos = jnp.tile(jnp.concatenate([cos, cos], axis=-1), (1, SWA_WIDTH // HEAD_DIM))
    sin = jnp.tile(jnp.concatenate([-sin, sin], axis=-1), (1, SWA_WIDTH // HEAD_DIM))
    return cos, sin


def _pad_lanes(v):
    return jnp.pad(v, (0, LANES - v.shape[0])).reshape(1, LANES)


def _layer_weights(l, w_in, dt_bias, A_log, D_skip, sinks, w_out, w_gate, w_up, w_down):
    wl = w_in[l]
    w_dt = jnp.pad(wl[:, _OFF_DT:_OFF_SQ], ((0, 0), (0, LANES - SSD_HEADS))).astype(BF16)
    head_params = jnp.concatenate(
        [_pad_lanes(dt_bias[l]), _pad_lanes(A_log[l]), _pad_lanes(D_skip[l]),
         jnp.zeros((SUBLANES - 3, LANES), F32)], axis=0)
    return dict(w_main=wl[:, :_OFF_DT].astype(BF16), w_swa=wl[:, _OFF_SQ:].astype(BF16), w_dt=w_dt,
                head_params=head_params, sinks=_pad_lanes(sinks[l]),
                w_out=w_out[l].astype(BF16), w_gate=w_gate[l].astype(BF16), w_up=w_up[l].astype(BF16),
                w_down=w_down[l].astype(BF16).reshape(N_EXPERTS * D_EXPERT, D_MODEL))


def _router_layout(router_w, router_bias):
    e = jnp.arange(N_EXPERTS)
    lanes = GATE_STRIDE * (e % EXPERTS_PER_GROUP) + e // EXPERTS_PER_GROUP
    w = jnp.zeros((D_MODEL, LANES), F32).at[:, lanes].set(router_w.astype(F32))
    b = jnp.zeros((1, LANES), F32).at[0, lanes].set(router_bias.astype(F32))
    w_hi = w.astype(BF16)
    w_lo = (w - w_hi.astype(F32)).astype(BF16)
    return w_hi, w_lo, b


def _trunk(x, mod, pos, past, layers, shared, *, tm, chunk, seqs, act_dtype, use_bf16):
    b, T, _ = x.shape
    n = b * T
    nc = T // chunk
    cos, sin = _rope_tables(pos)
    reps = tm // T if tm > T else 1
    cos3 = jnp.tile(cos, (reps, 1)).reshape(-1, tm, SWA_WIDTH)
    sin3 = jnp.tile(sin, (reps, 1)).reshape(-1, tm, SWA_WIDTH)
    x2d = x.reshape(n, D_MODEL)
    ret_all = ssd_all = None
    convs, ks, vs = [], [], []
    for l in range(DEPTH):
        lw = layers[l]
        ret, z, xbc, dt, sq, sk, sv = _inproj(
            x2d, shared["norm1_g"][l], mod[l], cos3, sin3, lw["w_main"], lw["w_swa"], lw["w_dt"],
            tm=tm, act_dtype=act_dtype)
        blocks = lambda a: a.reshape(b * nc, chunk, a.shape[-1])
        ret_state = _LayerState((DEPTH, b, RET_HEADS, RET_DK, RET_DV), l, seqs,
                                None if past is None else past[0], ret_all)
        ssd_state = _LayerState((DEPTH, b, SSD_HEADS, SSD_HEADDIM, SSD_DSTATE), l, seqs,
                                None if past is None else past[1], ssd_all)
        tail0 = None if past is None else jnp.pad(
            past[2][l], ((0, 0), (SUBLANES - (CONV_WIDTH - 1), 0), (0, 0)))
        ro, ret_all = _retention(blocks(ret), ret_state, nc=nc, seqs=seqs, out_dtype=act_dtype,
                                 use_bf16=use_bf16)
        ys, ssd_all = _ssd(blocks(xbc), blocks(z), blocks(dt), tail0, ssd_state, shared["conv_w"][l],
                           shared["conv_b"][l], lw["head_params"], shared["ssd_norm_g"][l],
                           nc=nc, seqs=seqs, out_dtype=act_dtype, use_bf16=use_bf16)
        sk3 = sk.reshape(b, T, SWA_KV_WIDTH)
        sv3 = sv.reshape(b, T, SWA_KV_WIDTH)
        if past is None:
            so = _swa_band(blocks(sq), blocks(sk), blocks(sv), lw["sinks"], nc=nc)
            keep = min(WINDOW, T)
            new_k, new_v = sk3[:, T - keep:], sv3[:, T - keep:]
        else:
            kc = past[3][l].reshape(b, -1, SWA_KV_WIDTH)
            vc = past[4][l].reshape(b, -1, SWA_KV_WIDTH)
            so = _swa_cache(sq.reshape(b, T, SWA_WIDTH), sk3, sv3, kc, vc, lw["sinks"], seqs=seqs)
            new_k = jnp.concatenate([kc[:, T:], sk3], axis=1)
            new_v = jnp.concatenate([vc[:, T:], sv3], axis=1)
        x1, h2, gates = _outproj(
            ro.reshape(n, RET_WIDTH), ys.reshape(n, D_INNER), so.reshape(n, SWA_WIDTH), x2d, mod[l],
            shared["norm2_g"][l], lw["w_out"], shared["wr_hi"], shared["wr_lo"], shared["router_bias"], tm=tm)
        x2d = _moe(h2, gates, x1, mod[l], lw["w_gate"], lw["w_up"], lw["w_down"], shared["final_g"],
                   tm=tm, final_norm=(l == DEPTH - 1))
        kv_shape = (b, -1, SWA_KV_HEADS, HEAD_DIM)
        convs.append(xbc.reshape(b, T, CONV_DIM)[:, T - (CONV_WIDTH - 1):])
        ks.append(new_k.reshape(kv_shape))
        vs.append(new_v.reshape(kv_shape))
    return x2d.reshape(b, T, D_MODEL), [ret_all, ssd_all, jnp.stack(convs), jnp.stack(ks), jnp.stack(vs)]


def kernel(x_prompt, x_sample, c_prompt, c_sample, state_ret, state_ssd, state_conv, cache_swa_k,
           cache_swa_v, ada_w, ada_b, norm1_g, norm2_g, w_in, conv_w, conv_b, dt_bias, A_log, D_skip,
           ssd_norm_g, sinks, w_out, router_w, router_bias, w_gate, w_up, w_down, final_g):
    bp, T, _ = x_prompt.shape
    bs, S, _ = x_sample.shape
    layers = [_layer_weights(l, w_in, dt_bias, A_log, D_skip, sinks, w_out, w_gate, w_up, w_down)
              for l in range(DEPTH)]
    wr_hi, wr_lo, rb = _router_layout(router_w, router_bias)
    shared = dict(norm1_g=norm1_g.reshape(DEPTH, 1, D_MODEL), norm2_g=norm2_g.reshape(DEPTH, 1, D_MODEL),
                  conv_w=conv_w, conv_b=conv_b.reshape(DEPTH, 1, CONV_DIM),
                  ssd_norm_g=ssd_norm_g.reshape(DEPTH, 1, D_INNER), wr_hi=wr_hi, wr_lo=wr_lo, router_bias=rb,
                  final_g=final_g.reshape(1, D_MODEL))
    mod = _modulation(jnp.concatenate([c_prompt, c_sample], axis=0), ada_w, ada_b)
    tm_s = bs * S
    mod_p = mod[:, :bp].reshape(DEPTH, bp, 1, 6 * D_MODEL)
    mod_s = jnp.repeat(mod[:, bp:], S, axis=1).reshape(DEPTH, 1, tm_s, 6 * D_MODEL)
    pos_p = jnp.arange(T, dtype=jnp.int32)
    pos_s = PAST_LEN + jnp.arange(S, dtype=jnp.int32)
    y_p, new_p = _trunk(x_prompt, mod_p, pos_p, None, layers, shared,
                        tm=TOKEN_TILE, chunk=CHUNK, seqs=1, act_dtype=BF16, use_bf16=True)
    y_s, new_s = _trunk(x_sample, mod_s, pos_s, (state_ret, state_ssd, state_conv, cache_swa_k, cache_swa_v),
                        layers, shared, tm=tm_s, chunk=S, seqs=SEQS_PER_STEP, act_dtype=F32, use_bf16=False)
    return (y_p, y_s, *new_p, *new_s)
```

```python
import functools

import jax
import jax.numpy as jnp
from jax import lax
from jax.experimental import pallas as pl
from jax.experimental.pallas import tpu as pltpu

F32 = jnp.float32
BF16 = jnp.bfloat16
HIGHEST = lax.Precision.HIGHEST

D_MODEL = 1024
DEPTH = 2
PAST_LEN = 16384
HEAD_DIM = 64
RET_HEADS = 4
RET_DK = 64
RET_DV = 64
RET_WIDTH = RET_HEADS * RET_DV
CHUNK = 128
ROPE_THETA = 10000.0
SSD_HEADS = 8
SSD_HEADDIM = 64
D_INNER = SSD_HEADS * SSD_HEADDIM
SSD_GROUPS = 2
SSD_DSTATE = 128
SSD_BC = SSD_GROUPS * SSD_DSTATE
CONV_WIDTH = 4
CONV_DIM = D_INNER + 2 * SSD_BC
SWA_HEADS = 4
SWA_KV_HEADS = 2
SWA_WIDTH = SWA_HEADS * HEAD_DIM
SWA_KV_WIDTH = SWA_KV_HEADS * HEAD_DIM
WINDOW = 128
N_EXPERTS = 16
N_EXPERT_GROUPS = 4
EXPERTS_PER_GROUP = N_EXPERTS // N_EXPERT_GROUPS
D_EXPERT = 256
NORM_EPS = 1e-6
LANES = 128
SUBLANES = 8
GATE_STRIDE = LANES // EXPERTS_PER_GROUP
VMEM_LIMIT = 56 * 1024 * 1024
TOKEN_TILE = 512
SEQS_PER_STEP = 8
SWA_HEAD_ORDER = (0, 2, 1, 3)

_OFF_XBC = 4 * RET_WIDTH + D_INNER
_OFF_DT = _OFF_XBC + CONV_DIM
_OFF_SQ = _OFF_DT + SSD_HEADS


def _silu(x):
    return x * (1.0 / (1.0 + jnp.exp(-x)))


def _dot(a, b, dims, use_bf16=True):
    if use_bf16:
        a = a.astype(BF16)
        b = b.astype(BF16)
    return lax.dot_general(a, b, (dims, ((), ())), preferred_element_type=F32)


_NN = ((1,), (0,))
_NT = ((1,), (1,))
_TN = ((0,), (0,))


def _split3(x, axis):
    t1 = x.astype(BF16)
    r1 = x - t1.astype(F32)
    t2 = r1.astype(BF16)
    t3 = (r1 - t2.astype(F32)).astype(BF16)
    return jnp.concatenate([t1, t2, t3], axis=axis)


def _sum3(y, axis):
    n = y.shape[axis] // 3
    parts = [lax.slice_in_dim(y, i * n, (i + 1) * n, axis=axis) for i in range(3)]
    return parts[0] + parts[1] + parts[2]


def _rows(parts):
    return parts[0] if len(parts) == 1 else jnp.concatenate(parts, axis=0)


def _params(sem):
    return pltpu.CompilerParams(dimension_semantics=sem, vmem_limit_bytes=VMEM_LIMIT)


def _const_spec(a):
    return pl.BlockSpec(a.shape, lambda *_: (0,) * a.ndim)


class _LayerState:
    def __init__(self, shape, layer, seqs, init, prev):
        self.shape, self.init, self.prev = shape, init, prev
        tail = (0,) * (len(shape) - 2)
        self.spec = pl.BlockSpec((None, seqs) + tuple(shape[2:]), lambda i, c: (layer, i) + tail)

    def add_inputs(self, operands, in_specs, aliases, out_index):
        if self.init is not None:
            operands.append(self.init)
            in_specs.append(self.spec)
        if self.prev is not None:
            aliases[len(operands)] = out_index
            operands.append(self.prev)
            in_specs.append(pl.BlockSpec(memory_space=pl.ANY))

    @property
    def out_shape(self):
        return jax.ShapeDtypeStruct(self.shape, F32)


class _Part:
    def __init__(self, body, operands, in_specs, out_specs, out_shape, scratch=(), aliases=None):
        self.body, self.operands, self.in_specs = body, operands, in_specs
        self.out_specs, self.out_shape, self.scratch = out_specs, out_shape, list(scratch)
        self.aliases = aliases or {}


def _run_parts(parts, grid, name):
    n_in = [len(p.operands) for p in parts]
    n_out = [len(p.out_shape) for p in parts]
    n_scr = [len(p.scratch) for p in parts]
    in_off = [sum(n_in[:i]) for i in range(len(parts))]
    out_off = [sum(n_out[:i]) for i in range(len(parts))]
    scr_off = [sum(n_scr[:i]) for i in range(len(parts))]
    aliases = {in_off[i] + a: out_off[i] + o for i, p in enumerate(parts) for a, o in p.aliases.items()}

    def kernel_fn(*refs):
        ins, outs = refs[:sum(n_in)], refs[sum(n_in):sum(n_in) + sum(n_out)]
        scr = refs[sum(n_in) + sum(n_out):]

        def run(phase):
            for i, p in enumerate(parts):
                p.body(ins[in_off[i]:in_off[i] + n_in[i]], outs[out_off[i]:out_off[i] + n_out[i]],
                       scr[scr_off[i]:scr_off[i] + n_scr[i]], phase)

        pl.when(pl.program_id(1) == 0)(lambda: run("init"))
        run("main")
        pl.when(pl.program_id(1) == pl.num_programs(1) - 1)(lambda: run("final"))

    res = pl.pallas_call(
        kernel_fn,
        grid=grid,
        in_specs=[s for p in parts for s in p.in_specs],
        out_specs=[s for p in parts for s in p.out_specs],
        out_shape=[s for p in parts for s in p.out_shape],
        scratch_shapes=[s for p in parts for s in p.scratch],
        input_output_aliases=aliases,
        compiler_params=_params(("arbitrary",) * len(grid)),
        name=name,
    )(*[o for p in parts for o in p.operands])
    return [res[out_off[i]:out_off[i] + n_out[i]] for i in range(len(parts))]


def _mod_kernel(c_ref, w_ref, b_ref, o_ref):
    c = c_ref[...]
    o_ref[...] = jnp.dot(_silu(c), w_ref[...], precision=HIGHEST,
                         preferred_element_type=F32) + b_ref[...]


def _modulation(c_all, ada_w, ada_b):
    rows = c_all.shape[0]
    tn = 512
    return pl.pallas_call(
        _mod_kernel,
        grid=(DEPTH, 6 * D_MODEL // tn),
        in_specs=[
            pl.BlockSpec((rows, D_MODEL), lambda l, j: (0, 0)),
            pl.BlockSpec((None, D_MODEL, tn), lambda l, j: (l, 0, j)),
            pl.BlockSpec((None, 1, tn), lambda l, j: (l, 0, j)),
        ],
        out_specs=pl.BlockSpec((None, rows, tn), lambda l, j: (l, 0, j)),
        out_shape=jax.ShapeDtypeStruct((DEPTH, rows, 6 * D_MODEL), F32),
        compiler_params=_params(("arbitrary", "arbitrary")),
        name="modulation",
    )(c_all, ada_w, ada_b.reshape(DEPTH, 1, 6 * D_MODEL))


def _rotate(p, cos, sin_signed):
    n = p.shape[-1]
    lane = lax.broadcasted_iota(jnp.int32, p.shape, 1)
    first_half = (lane % HEAD_DIM) < (HEAD_DIM // 2)
    partner = jnp.where(first_half, pltpu.roll(p, n - HEAD_DIM // 2, 1),
                        pltpu.roll(p, HEAD_DIM // 2, 1))
    return p * cos + partner * sin_signed


def _inproj_kernel(x_ref, g_ref, sc_ref, sh_ref, cos_ref, sin_ref, w_ref, ws_ref, wdt_ref,
                   ret_ref, z_ref, xbc_ref, dt_ref, sq_ref, sk_ref, sv_ref):
    x = x_ref[...]
    xn = x * lax.rsqrt(jnp.mean(x * x, axis=-1, keepdims=True) + NORM_EPS) * g_ref[...]
    h = (xn * (1.0 + sc_ref[...]) + sh_ref[...]).astype(BF16)
    cos = cos_ref[...]
    sin = sin_ref[...]

    def proj(ref, lo, width):
        return jnp.dot(h, ref[:, lo:lo + width], preferred_element_type=F32)

    w = RET_WIDTH
    ret_ref[:, 0:w] = _rotate(proj(w_ref, 0, w), cos, sin).astype(ret_ref.dtype)
    ret_ref[:, w:2 * w] = (_rotate(proj(w_ref, w, w), cos, sin) * RET_DK ** -0.5).astype(ret_ref.dtype)
    ret_ref[:, 2 * w:4 * w] = proj(w_ref, 2 * w, 2 * w).astype(ret_ref.dtype)
    z_ref[...] = proj(w_ref, 4 * w, D_INNER).astype(z_ref.dtype)
    xbc_ref[...] = proj(w_ref, _OFF_XBC, CONV_DIM)
    dt_ref[...] = jnp.dot(h, wdt_ref[...], preferred_element_type=F32)
    sq_ref[...] = (_rotate(proj(ws_ref, 0, SWA_WIDTH), cos, sin) * HEAD_DIM ** -0.5).astype(sq_ref.dtype)
    sk_ref[...] = _rotate(proj(ws_ref, SWA_WIDTH, SWA_KV_WIDTH), cos[:, :SWA_KV_WIDTH], sin[:, :SWA_KV_WIDTH])
    sv_ref[...] = proj(ws_ref, SWA_WIDTH + SWA_KV_WIDTH, SWA_KV_WIDTH)


def _inproj(x2d, gain, mod3, cos3, sin3, w_main, w_swa, w_dt, *, tm, act_dtype):
    n = x2d.shape[0]
    nt = n // tm
    per_mod = nt // mod3.shape[0]
    npos = cos3.shape[0]
    r = mod3.shape[1]
    row = lambda i: (i, 0)
    outs = [
        (4 * RET_WIDTH, act_dtype), (D_INNER, act_dtype), (CONV_DIM, F32), (LANES, F32),
        (SWA_WIDTH, act_dtype), (SWA_KV_WIDTH, F32), (SWA_KV_WIDTH, F32),
    ]
    return pl.pallas_call(
        _inproj_kernel,
        grid=(nt,),
        in_specs=[
            pl.BlockSpec((tm, D_MODEL), row),
            _const_spec(gain),
            pl.BlockSpec((None, r, D_MODEL), lambda i: (i // per_mod, 0, 1)),
            pl.BlockSpec((None, r, D_MODEL), lambda i: (i // per_mod, 0, 0)),
            pl.BlockSpec((None, tm, SWA_WIDTH), lambda i: (i % npos, 0, 0)),
            pl.BlockSpec((None, tm, SWA_WIDTH), lambda i: (i % npos, 0, 0)),
            _const_spec(w_main), _const_spec(w_swa), _const_spec(w_dt),
        ],
        out_specs=[pl.BlockSpec((tm, c), row) for c, _ in outs],
        out_shape=[jax.ShapeDtypeStruct((n, c), dt) for c, dt in outs],
        compiler_params=_params(("arbitrary",)),
        name="inproj",
    )(x2d, gain, mod3, mod3, cos3, sin3, w_main, w_swa, w_dt)


def _lane_masks(width, group, count, dtype):
    lane = lax.broadcasted_iota(jnp.int32, (1, width), 1) // group
    return [(lane == i).astype(F32).astype(dtype) for i in range(count)]


def _stack_masked(x, masks):
    return jnp.concatenate([x * m for m in masks], axis=0)


def _unstack_masked(y, masks):
    L = y.shape[0] // len(masks)
    out = y[0:L] * masks[0]
    for i in range(1, len(masks)):
        out = out + y[i * L:(i + 1) * L] * masks[i]
    return out


def _ret_tables(L):
    log_gamma = jnp.log(1.0 - 2.0 ** (-5.0 - jnp.arange(RET_HEADS, dtype=F32)))
    idx = jnp.arange(L, dtype=F32)
    diff = idx[:, None] - idx[None, :]
    decay = jnp.where(diff >= 0, jnp.exp(jnp.maximum(diff, 0.0)[None] * log_gamma[:, None, None]), 0.0)
    lane_head = jnp.arange(RET_WIDTH) // RET_DK
    lg = log_gamma[lane_head]
    same = lane_head[:, None] == lane_head[None, :]
    chunk_decay = jnp.exp(L * log_gamma)[lane_head]
    return [
        decay.reshape(RET_HEADS * L, L),
        jnp.exp((idx + 1.0)[:, None] * lg[None, :]),
        jnp.exp((L - 1.0 - idx)[:, None] * lg[None, :]),
        jnp.where(same, chunk_decay[:, None], 0.0),
        same.astype(F32),
    ]


def _ret_body(ins, outs, scratch, phase, *, has_init, use_bf16):
    blk_ref, decay_ref, qd_ref, kd_ref, sd_ref, bd_ref = ins[:6]
    s0_ref = ins[6] if has_init else None
    ro_ref, s_ref = outs
    sbd_ref, = scratch
    seqs, L = blk_ref.shape[0], blk_ref.shape[1]
    w, dk = RET_WIDTH, RET_DK

    if phase == "init":
        sbd_ref[...] = jnp.zeros(sbd_ref.shape, F32)
        if has_init:
            for j in range(seqs):
                for h in range(RET_HEADS):
                    sbd_ref[j, h * dk:(h + 1) * dk, h * dk:(h + 1) * dk] = s0_ref[j, h]
        return
    if phase == "final":
        for j in range(seqs):
            for h in range(RET_HEADS):
                s_ref[j, h] = sbd_ref[j, h * dk:(h + 1) * dk, h * dk:(h + 1) * dk]
        return

    block_diag = bd_ref[...]
    o_all, gate_all = [], []
    for j in range(seqs):
        blk = blk_ref[j]
        q, k, v = blk[:, 0:w], blk[:, w:2 * w], blk[:, 2 * w:3 * w]
        gate_all.append(blk[:, 3 * w:4 * w].astype(F32))
        masks = _lane_masks(w, dk, RET_HEADS, q.dtype)
        att = _dot(_stack_masked(q, masks), k, _NT, use_bf16) * decay_ref[...]
        o = _unstack_masked(_dot(att, v, _NN, use_bf16), _lane_masks(w, dk, RET_HEADS, F32))
        s = sbd_ref[j]
        o_all.append(o + _dot(q, s, _NN, use_bf16) * qd_ref[...])
        sbd_ref[j] = s * sd_ref[...] + _dot(k.astype(F32) * kd_ref[...], v, _TN, use_bf16) * block_diag
    o = _rows(o_all)
    mean = _dot(o, block_diag, _NN, use_bf16) * (1.0 / RET_DV)
    cen = o - mean
    var = _dot(cen * cen, block_diag, _NN, use_bf16) * (1.0 / RET_DV)
    out = _silu(_rows(gate_all)) * (cen * lax.rsqrt(var + NORM_EPS))
    for j in range(seqs):
        ro_ref[j] = out[j * L:(j + 1) * L].astype(ro_ref.dtype)


def _retention_part(ret3, state, *, nc, seqs, out_dtype, use_bf16):
    nblk, L, _ = ret3.shape
    tables = _ret_tables(L)
    blk = lambda c: pl.BlockSpec((seqs, L, c), lambda i, j: (i * nc + j, 0, 0))
    operands = [ret3] + tables
    in_specs = [blk(4 * RET_WIDTH)] + [_const_spec(a) for a in tables]
    aliases = {}
    state.add_inputs(operands, in_specs, aliases, 1)
    return _Part(
        functools.partial(_ret_body, has_init=state.init is not None, use_bf16=use_bf16),
        operands, in_specs, [blk(RET_WIDTH), state.spec],
        [jax.ShapeDtypeStruct((nblk, L, RET_WIDTH), out_dtype), state.out_shape],
        [pltpu.VMEM((seqs, RET_WIDTH, RET_WIDTH), F32)], aliases)


def _softplus(x):
    return jnp.maximum(x, 0.0) + jnp.log1p(jnp.exp(-jnp.abs(x)))


def _ssd_expand_table():
    src = jnp.arange(LANES)
    dst = jnp.arange(2 * SSD_HEADS * LANES)
    expand = (dst[None, :] // LANES == src[:, None]) & (src[:, None] < 2 * SSD_HEADS)
    return jnp.tile(expand.astype(BF16), (3, 1))


def _per_head64(e):
    lane = lax.broadcasted_iota(jnp.int32, (1, LANES), 1)
    low = lane < SSD_HEADDIM
    cols = [jnp.where(low, e[:, (2 * c) * LANES:(2 * c + 1) * LANES], e[:, (2 * c + 1) * LANES:(2 * c + 2) * LANES])
            for c in range(SSD_HEADS // 2)]
    return jnp.concatenate(cols, axis=-1)


def _ssd_body(ins, outs, scratch, phase, *, has_init, use_bf16):
    xbc_ref, z_ref, dt_ref, cw_ref, cb_ref, hp_ref, dskip_ref, ng_ref, expand_ref = ins[:9]
    tail0_ref, h0_ref = (ins[9], ins[10]) if has_init else (None, None)
    ys_ref, h_ref = outs
    tail_ref, hbd_ref = scratch
    seqs, L = xbc_ref.shape[0], xbc_ref.shape[1]
    R = seqs * L
    keep = CONV_WIDTH - 1
    n, p = SSD_DSTATE, SSD_HEADDIM
    per_group = SSD_HEADS // SSD_GROUPS

    if phase == "init":
        hbd_ref[...] = jnp.zeros(hbd_ref.shape, F32)
        tail_ref[...] = tail0_ref[...] if has_init else jnp.zeros(tail_ref.shape, F32)
        if has_init:
            for j in range(seqs):
                for h in range(SSD_HEADS):
                    g = h // per_group
                    hbd_ref[j, h * p:(h + 1) * p, g * n:(g + 1) * n] = h0_ref[j, h]
        return
    if phase == "final":
        for j in range(seqs):
            for h in range(SSD_HEADS):
                g = h // per_group
                h_ref[j, h] = hbd_ref[j, h * p:(h + 1) * p, g * n:(g + 1) * n]
        return

    cw = cw_ref[...]
    hp = hp_ref[...]
    lane = lax.broadcasted_iota(jnp.int32, (1, LANES), 1)
    gmask = _lane_masks(SSD_BC, n, SSD_GROUPS, F32)
    hmask = _lane_masks(per_group * p, p, per_group, F32)

    xs_l, bs_l, cs_l = [], [], []
    for j in range(seqs):
        cur = xbc_ref[j]
        ext = jnp.concatenate([tail_ref[j], cur], axis=0)
        xc = cw[keep:keep + 1] * cur
        for wi in range(keep):
            lo = SUBLANES - keep + wi
            xc = xc + cw[wi:wi + 1] * ext[lo:lo + L]
        if L >= SUBLANES:
            tail_ref[j] = cur[L - SUBLANES:]
        xc = _silu(xc + cb_ref[...])
        xs_l.append(xc[:, :D_INNER])
        bs_l.append(xc[:, D_INNER:D_INNER + SSD_BC])
        cs_l.append(xc[:, D_INNER + SSD_BC:])

    dtv = _softplus(_rows([dt_ref[j] for j in range(seqs)]) + hp[0:1])
    a = dtv * (-jnp.exp(hp[1:2]))
    row = lax.broadcasted_iota(jnp.int32, (R, R), 0)
    col = lax.broadcasted_iota(jnp.int32, (R, R), 1)
    causal = (row >= col) & (row // L == col // L)
    acum = _sum3(jnp.dot(causal.astype(F32).astype(BF16), _split3(a, 1), preferred_element_type=F32), 1)
    pick = (lax.broadcasted_iota(jnp.int32, (SUBLANES, LANES), 0)
            == lax.broadcasted_iota(jnp.int32, (SUBLANES, LANES), 1)).astype(F32).astype(BF16)
    acum_t = _sum3(lax.dot_general(pick, _split3(acum, 0), (_NT, ((), ())),
                                   preferred_element_type=F32), 1)
    both = jnp.where(lane < SSD_HEADS, dtv, pltpu.roll(acum, SSD_HEADS, 1))
    spread = jnp.dot(_split3(both, 1), expand_ref[...], preferred_element_type=F32)
    acum_e = spread[:, SSD_HEADS * LANES:]
    dt64 = _per_head64(spread[:, :SSD_HEADS * LANES])
    acum64 = _per_head64(acum_e)
    tri = (lax.broadcasted_iota(jnp.int32, (L, L), 0) >= lax.broadcasted_iota(jnp.int32, (L, L), 1))

    y_l = []
    for j in range(seqs):
        r0 = j * L
        xs, bs, cs = xs_l[j], bs_l[j], cs_l[j]
        a64, d64 = acum64[r0:r0 + L], dt64[r0:r0 + L]
        cb = _dot(_stack_masked(cs, gmask), bs, _NT, use_bf16)
        xdt = xs * d64
        y_parts = []
        for g in range(SSD_GROUPS):
            atts = []
            for h in range(g * per_group, (g + 1) * per_group):
                a_col = acum_e[r0:r0 + L, h * LANES:h * LANES + L]
                seg = jnp.exp(jnp.where(tri, a_col - acum_t[h:h + 1, r0:r0 + L], -jnp.inf))
                atts.append(cb[g * L:(g + 1) * L] * seg)
            yg = _dot(jnp.concatenate(atts, axis=0), xdt[:, g * per_group * p:(g + 1) * per_group * p],
                      _NN, use_bf16)
            y_parts.append(_unstack_masked(yg, hmask))
        state = hbd_ref[j]
        y_l.append(jnp.concatenate(y_parts, axis=-1) + _dot(cs, state, _NT, use_bf16) * jnp.exp(a64)
                   + dskip_ref[...] * xs)
        w_end = jnp.exp(a64[L - 1:L] - a64) * d64
        kv = _dot(xs * w_end, bs, _TN, use_bf16)
        for h in range(SSD_HEADS):
            g = h // per_group
            carry = jnp.exp(acum_t[h:h + 1, r0 + L - 1:r0 + L])
            hbd_ref[j, h * p:(h + 1) * p, g * n:(g + 1) * n] = (
                carry * state[h * p:(h + 1) * p, g * n:(g + 1) * n] + kv[h * p:(h + 1) * p, g * n:(g + 1) * n])

    ys = _rows(y_l) * _silu(_rows([z_ref[j] for j in range(seqs)]).astype(F32))
    ys = ys * lax.rsqrt(jnp.mean(ys * ys, axis=-1, keepdims=True) + NORM_EPS) * ng_ref[...]
    for j in range(seqs):
        ys_ref[j] = ys[j * L:(j + 1) * L].astype(ys_ref.dtype)


def _ssd_part(xbc3, z3, dt3, tail0, state, conv_w, conv_b, head_params, dskip64, norm_g, *, nc, seqs, out_dtype,
              use_bf16):
    nblk, L, _ = xbc3.shape
    assert L >= SUBLANES or nc == 1, "the conv tail is only carried between chunks of at least 8 rows"
    blk = lambda c: pl.BlockSpec((seqs, L, c), lambda i, j: (i * nc + j, 0, 0))
    consts = [conv_w, conv_b, head_params, dskip64, norm_g, _ssd_expand_table()]
    operands = [xbc3, z3, dt3] + consts
    in_specs = [blk(CONV_DIM), blk(D_INNER), blk(LANES)] + [_const_spec(a) for a in consts]
    aliases = {}
    if state.init is not None:
        operands.append(tail0)
        in_specs.append(pl.BlockSpec((seqs, SUBLANES, CONV_DIM), lambda i, j: (i, 0, 0)))
    state.add_inputs(operands, in_specs, aliases, 1)
    return _Part(
        functools.partial(_ssd_body, has_init=state.init is not None, use_bf16=use_bf16),
        operands, in_specs, [blk(D_INNER), state.spec],
        [jax.ShapeDtypeStruct((nblk, L, D_INNER), out_dtype), state.out_shape],
        [pltpu.VMEM((seqs, SUBLANES, CONV_DIM), F32), pltpu.VMEM((seqs, D_INNER, SSD_BC), F32)], aliases)


def _swa_stack_queries(q):
    lo, hi = _lane_masks(SWA_KV_WIDTH, HEAD_DIM, SWA_KV_HEADS, q.dtype)
    c0, c1 = q[:, :SWA_KV_WIDTH], q[:, SWA_KV_WIDTH:]
    return jnp.concatenate([c0 * lo, c1 * lo, c0 * hi, c1 * hi], axis=0)


def _swa_unstack(o):
    L = o.shape[0] // SWA_HEADS
    lo, hi = _lane_masks(SWA_KV_WIDTH, HEAD_DIM, SWA_KV_HEADS, F32)
    return jnp.concatenate([o[0:L] * lo + o[2 * L:3 * L] * hi, o[L:2 * L] * lo + o[3 * L:4 * L] * hi], axis=-1)


def _sink_column(sinks, L):
    return jnp.concatenate([jnp.broadcast_to(sinks[:, h:h + 1], (L, 1)) for h in range(SWA_HEADS)], axis=0)


def _sink_softmax_pv(parts, sink):
    m = sink
    for s, _ in parts:
        m = jnp.maximum(m, jnp.max(s, axis=-1, keepdims=True))
    acc = None
    for s, v in parts:
        v_ext = jnp.concatenate([v.astype(BF16), jnp.ones(v.shape, BF16)], axis=-1)
        pv = _dot(jnp.exp(s - m), v_ext, _NN)
        acc = pv if acc is None else acc + pv
    width = parts[0][1].shape[-1]
    return acc[:, :width] / (acc[:, width:] + jnp.exp(sink - m))


def _swa_band_body(ins, outs, scratch, phase):
    if phase != "main":
        return
    q_ref, kp_ref, kc_ref, vp_ref, vc_ref, sink_ref = ins
    o_ref, = outs
    L = q_ref.shape[1]
    c = pl.program_id(1)
    keys = jnp.concatenate([kp_ref[0], kc_ref[0]], axis=0).astype(BF16)
    vals = jnp.concatenate([vp_ref[0], vc_ref[0]], axis=0)
    row = lax.broadcasted_iota(jnp.int32, (SWA_HEADS * L, 2 * L), 0) % L
    col = lax.broadcasted_iota(jnp.int32, (SWA_HEADS * L, 2 * L), 1)
    rel = row - col + L
    valid = (rel >= 0) & (rel < WINDOW) & ((col >= L) | (c > 0))
    s = jnp.where(valid, _dot(_swa_stack_queries(q_ref[0]), keys, _NT), -1e30)
    o = _sink_softmax_pv([(s, vals)], _sink_column(sink_ref[...], L))
    o_ref[0] = _swa_unstack(o).astype(o_ref.dtype)


def _swa_band_part(sq3, sk3, sv3, sinks_row, *, nc):
    nblk, L, _ = sq3.shape
    cur = lambda c: pl.BlockSpec((1, L, c), lambda i, j: (i * nc + j, 0, 0))
    prev = lambda c: pl.BlockSpec((1, L, c), lambda i, j: (i * nc + jnp.maximum(j - 1, 0), 0, 0))
    return _Part(
        _swa_band_body, [sq3, sk3, sk3, sv3, sv3, sinks_row],
        [cur(SWA_WIDTH), prev(SWA_KV_WIDTH), cur(SWA_KV_WIDTH), prev(SWA_KV_WIDTH), cur(SWA_KV_WIDTH),
         _const_spec(sinks_row)],
        [cur(SWA_WIDTH)], [jax.ShapeDtypeStruct((nblk, L, SWA_WIDTH), BF16)])


def _swa_cache_body(ins, outs, scratch, phase):
    if phase != "main":
        return
    q_ref, kn_ref, vn_ref, kc_ref, vc_ref, sink_ref = ins
    o_ref, = outs
    seqs, T = q_ref.shape[0], q_ref.shape[1]
    W = kc_ref.shape[1]
    rows = SWA_HEADS * T
    qt = lax.broadcasted_iota(jnp.int32, (rows, W), 0) % T
    kcol = lax.broadcasted_iota(jnp.int32, (rows, W), 1)
    rel_c = qt + W - kcol
    valid_c = (rel_c >= 0) & (rel_c < WINDOW) & (kcol + (PAST_LEN - W) >= 0)
    rel_n = (lax.broadcasted_iota(jnp.int32, (rows, T), 0) % T) - lax.broadcasted_iota(jnp.int32, (rows, T), 1)
    valid_n = (rel_n >= 0) & (rel_n < WINDOW)
    sink = _sink_column(sink_ref[...], T)
    for j in range(seqs):
        qs = _swa_stack_queries(q_ref[j])
        s_c = jnp.where(valid_c, _dot(qs, kc_ref[j], _NT), -1e30)
        s_n = jnp.where(valid_n, _dot(qs, kn_ref[j], _NT), -1e30)
        o = _sink_softmax_pv([(s_c, vc_ref[j]), (s_n, vn_ref[j])], sink)
        o_ref[j] = _swa_unstack(o).astype(o_ref.dtype)


def _swa_cache_part(sq3, sk3, sv3, k_cache, v_cache, sinks_row, *, seqs):
    b, T, _ = sq3.shape
    W = k_cache.shape[1]
    new = lambda c: pl.BlockSpec((seqs, T, c), lambda i, j: (i, 0, 0))
    cache = pl.BlockSpec((seqs, W, SWA_KV_WIDTH), lambda i, j: (i, 0, 0))
    return _Part(
        _swa_cache_body, [sq3, sk3, sv3, k_cache, v_cache, sinks_row],
        [new(SWA_WIDTH), new(SWA_KV_WIDTH), new(SWA_KV_WIDTH), cache, cache, _const_spec(sinks_row)],
        [new(SWA_WIDTH)], [jax.ShapeDtypeStruct((b, T, SWA_WIDTH), F32)])


def _router_gates(hn, h_hi, wr_hi_ref, wr_lo_ref, rb_ref):
    h_lo = (hn - h_hi.astype(F32)).astype(BF16)
    logits = (jnp.dot(h_hi, wr_hi_ref[...], preferred_element_type=F32)
              + jnp.dot(h_lo, wr_hi_ref[...], preferred_element_type=F32)
              + jnp.dot(h_hi, wr_lo_ref[...], preferred_element_type=F32))
    scores = 1.0 / (1.0 + jnp.exp(-logits))
    biased = scores + rb_ref[...]
    lane = lax.broadcasted_iota(jnp.int32, logits.shape, 1)
    group_lane = lane < N_EXPERT_GROUPS
    sk = [scores] + [pltpu.roll(scores, LANES - GATE_STRIDE * k, 1) for k in range(1, EXPERTS_PER_GROUP)]
    bk = [biased] + [pltpu.roll(biased, LANES - GATE_STRIDE * k, 1) for k in range(1, EXPERTS_PER_GROUP)]
    hi1, lo1 = jnp.maximum(bk[0], bk[1]), jnp.minimum(bk[0], bk[1])
    hi2, lo2 = jnp.maximum(bk[2], bk[3]), jnp.minimum(bk[2], bk[3])
    top1 = jnp.maximum(hi1, hi2)
    top2 = jnp.maximum(jnp.minimum(hi1, hi2), jnp.maximum(lo1, lo2))
    group_score = jnp.where(group_lane, top1 + top2, -jnp.inf)
    best_score = jnp.max(group_score, axis=-1, keepdims=True)
    best = jnp.min(jnp.where(group_score == best_score, lane, LANES), axis=-1, keepdims=True)
    in_group = lane == best
    picked = []
    for k in range(EXPERTS_PER_GROUP):
        rank = jnp.zeros(logits.shape, jnp.int32)
        for j in range(EXPERTS_PER_GROUP):
            if j == k:
                continue
            ahead = (bk[j] >= bk[k]) if j < k else (bk[j] > bk[k])
            rank = rank + ahead.astype(jnp.int32)
        picked.append(jnp.where((rank < 2) & in_group, sk[k], 0.0))
    total = jnp.sum(picked[0] + picked[1] + picked[2] + picked[3], axis=-1, keepdims=True)
    gates = picked[0] / total
    for k in range(1, EXPERTS_PER_GROUP):
        gates = gates + pltpu.roll(picked[k] / total, GATE_STRIDE * k, 1)
    return gates


def _outproj_kernel(ro_ref, ys_ref, so_ref, x_ref, g1_ref, sc_ref, sh_ref, n2_ref, w_ref, wr_hi_ref, wr_lo_ref,
                    rb_ref, x1_ref, h2_ref, gate_ref):
    m = jnp.dot(ro_ref[...].astype(BF16), w_ref[0:RET_WIDTH, :], preferred_element_type=F32)
    m = m + jnp.dot(ys_ref[...].astype(BF16), w_ref[RET_WIDTH:RET_WIDTH + D_INNER, :],
                    preferred_element_type=F32)
    m = m + jnp.dot(so_ref[...].astype(BF16), w_ref[RET_WIDTH + D_INNER:, :], preferred_element_type=F32)
    x1 = x_ref[...] + g1_ref[...] * m
    x1_ref[...] = x1
    xn = x1 * lax.rsqrt(jnp.mean(x1 * x1, axis=-1, keepdims=True) + NORM_EPS) * n2_ref[...]
    hn = xn * (1.0 + sc_ref[...]) + sh_ref[...]
    h_hi = hn.astype(BF16)
    h2_ref[...] = h_hi
    gate_ref[...] = _router_gates(hn, h_hi, wr_hi_ref, wr_lo_ref, rb_ref)


def _outproj(ro, ys, so, x2d, mod3, gain2, w_out, wr_hi, wr_lo, router_bias, *, tm):
    n = x2d.shape[0]
    nt = n // tm
    per_mod = nt // mod3.shape[0]
    r = mod3.shape[1]
    row = lambda c: pl.BlockSpec((tm, c), lambda i: (i, 0))
    mod = lambda j: pl.BlockSpec((None, r, D_MODEL), lambda i: (i // per_mod, 0, j))
    return pl.pallas_call(
        _outproj_kernel,
        grid=(nt,),
        in_specs=[row(RET_WIDTH), row(D_INNER), row(SWA_WIDTH), row(D_MODEL),
                  mod(2), mod(4), mod(3), _const_spec(gain2), _const_spec(w_out), _const_spec(wr_hi),
                  _const_spec(wr_lo), _const_spec(router_bias)],
        out_specs=[row(D_MODEL), row(D_MODEL), row(LANES)],
        out_shape=[jax.ShapeDtypeStruct((n, D_MODEL), F32), jax.ShapeDtypeStruct((n, D_MODEL), BF16),
                   jax.ShapeDtypeStruct((n, LANES), F32)],
        compiler_params=_params(("arbitrary",)),
        name="outproj",
    )(ro, ys, so, x2d, mod3, mod3, mod3, gain2, w_out, wr_hi, wr_lo, router_bias)


def _moe_kernel(h_ref, gate_ref, x1_ref, g2_ref, wg_ref, wu_ref, wd_ref, fg_ref, o_ref, *, final_norm):
    h = h_ref[...]
    gates = gate_ref[...]
    acts = []
    for e in range(N_EXPERTS):
        lane = GATE_STRIDE * (e % EXPERTS_PER_GROUP) + e // EXPERTS_PER_GROUP
        g = jnp.dot(h, wg_ref[e], preferred_element_type=F32)
        u = jnp.dot(h, wu_ref[e], preferred_element_type=F32)
        acts.append((_silu(g) * u * gates[:, lane:lane + 1]).astype(BF16))
    mixed = jnp.dot(jnp.concatenate(acts, axis=-1), wd_ref[...], preferred_element_type=F32)
    x2 = x1_ref[...] + g2_ref[...] * mixed
    if final_norm:
        x2 = x2 * lax.rsqrt(jnp.mean(x2 * x2, axis=-1, keepdims=True) + NORM_EPS) * fg_ref[...]
    o_ref[...] = x2


def _moe(h2, gates, x1, mod3, w_gate, w_up, w_down, final_g, *, tm, final_norm):
    n = h2.shape[0]
    nt = n // tm
    per_mod = nt // mod3.shape[0]
    r = mod3.shape[1]
    row = lambda c: pl.BlockSpec((tm, c), lambda i: (i, 0))
    return pl.pallas_call(
        functools.partial(_moe_kernel, final_norm=final_norm),
        grid=(nt,),
        in_specs=[row(D_MODEL), row(LANES), row(D_MODEL),
                  pl.BlockSpec((None, r, D_MODEL), lambda i: (i // per_mod, 0, 5)),
                  _const_spec(w_gate), _const_spec(w_up), _const_spec(w_down), _const_spec(final_g)],
        out_specs=row(D_MODEL),
        out_shape=jax.ShapeDtypeStruct((n, D_MODEL), F32),
        compiler_params=_params(("arbitrary",)),
        name="moe",
    )(h2, gates, x1, mod3, w_gate, w_up, w_down, final_g)


def _rope_tables(pos):
    half = HEAD_DIM // 2
    inv = ROPE_THETA ** (-jnp.arange(half, dtype=F32) / half)
    ang = pos.astype(F32)[:, None] * inv[None, :]
    cos = jnp.cos(ang)
    sin = jnp.sin(ang)
    cos = jnp.tile(jnp.concatenate([cos, cos], axis=-1), (1, SWA_WIDTH // HEAD_DIM))
    sin = jnp.tile(jnp.concatenate([-sin, sin], axis=-1), (1, SWA_WIDTH // HEAD_DIM))
    return cos, sin


def _pad_lanes(v):
    return jnp.pad(v, (0, LANES - v.shape[0])).reshape(1, LANES)


def _swa_head_permute(w, axis):
    blocks = [lax.slice_in_dim(w, h * HEAD_DIM, (h + 1) * HEAD_DIM, axis=axis) for h in SWA_HEAD_ORDER]
    return jnp.concatenate(blocks, axis=axis)


def _layer_weights(l, w_in, dt_bias, A_log, D_skip, sinks, w_out, w_gate, w_up, w_down):
    wl = w_in[l]
    w_dt = jnp.pad(wl[:, _OFF_DT:_OFF_SQ], ((0, 0), (0, LANES - SSD_HEADS))).astype(BF16)
    head_params = jnp.concatenate(
        [_pad_lanes(dt_bias[l]), _pad_lanes(A_log[l]), jnp.zeros((SUBLANES - 2, LANES), F32)], axis=0)
    w_swa = jnp.concatenate([_swa_head_permute(wl[:, _OFF_SQ:_OFF_SQ + SWA_WIDTH], 1),
                             wl[:, _OFF_SQ + SWA_WIDTH:]], axis=1)
    wo = w_out[l]
    w_o = jnp.concatenate([wo[:RET_WIDTH + D_INNER], _swa_head_permute(wo[RET_WIDTH + D_INNER:], 0)], axis=0)
    return dict(w_main=wl[:, :_OFF_DT].astype(BF16), w_swa=w_swa.astype(BF16), w_dt=w_dt,
                head_params=head_params, dskip64=jnp.repeat(D_skip[l].astype(F32), SSD_HEADDIM).reshape(1, D_INNER),
                sinks=_pad_lanes(sinks[l]),
                w_out=w_o.astype(BF16), w_gate=w_gate[l].astype(BF16), w_up=w_up[l].astype(BF16),
                w_down=w_down[l].astype(BF16).reshape(N_EXPERTS * D_EXPERT, D_MODEL))


def _router_layout(router_w, router_bias):
    e = jnp.arange(N_EXPERTS)
    lanes = GATE_STRIDE * (e % EXPERTS_PER_GROUP) + e // EXPERTS_PER_GROUP
    w = jnp.zeros((D_MODEL, LANES), F32).at[:, lanes].set(router_w.astype(F32))
    b = jnp.zeros((1, LANES), F32).at[0, lanes].set(router_bias.astype(F32))
    w_hi = w.astype(BF16)
    w_lo = (w - w_hi.astype(F32)).astype(BF16)
    return w_hi, w_lo, b


def _trunk(x, mod, pos, past, layers, shared, *, tm, chunk, seqs, act_dtype, use_bf16):
    b, T, _ = x.shape
    n = b * T
    nc = T // chunk
    cos, sin = _rope_tables(pos)
    reps = tm // T if tm > T else 1
    cos3 = jnp.tile(cos, (reps, 1)).reshape(-1, tm, SWA_WIDTH)
    sin3 = jnp.tile(sin, (reps, 1)).reshape(-1, tm, SWA_WIDTH)
    x2d = x.reshape(n, D_MODEL)
    ret_all = ssd_all = None
    convs, ks, vs = [], [], []
    for l in range(DEPTH):
        lw = layers[l]
        ret, z, xbc, dt, sq, sk, sv = _inproj(
            x2d, shared["norm1_g"][l], mod[l], cos3, sin3, lw["w_main"], lw["w_swa"], lw["w_dt"],
            tm=tm, act_dtype=act_dtype)
        blocks = lambda a: a.reshape(b * nc, chunk, a.shape[-1])
        ret_state = _LayerState((DEPTH, b, RET_HEADS, RET_DK, RET_DV), l, seqs,
                                None if past is None else past[0], ret_all)
        ssd_state = _LayerState((DEPTH, b, SSD_HEADS, SSD_HEADDIM, SSD_DSTATE), l, seqs,
                                None if past is None else past[1], ssd_all)
        tail0 = None if past is None else jnp.pad(
            past[2][l], ((0, 0), (SUBLANES - (CONV_WIDTH - 1), 0), (0, 0)))
        parts = [
            _retention_part(blocks(ret), ret_state, nc=nc, seqs=seqs, out_dtype=act_dtype, use_bf16=use_bf16),
            _ssd_part(blocks(xbc), blocks(z), blocks(dt), tail0, ssd_state, shared["conv_w"][l],
                      shared["conv_b"][l], lw["head_params"], lw["dskip64"], shared["ssd_norm_g"][l],
                      nc=nc, seqs=seqs, out_dtype=act_dtype, use_bf16=use_bf16),
        ]
        sk3 = sk.reshape(b, T, SWA_KV_WIDTH)
        sv3 = sv.reshape(b, T, SWA_KV_WIDTH)
        if past is None:
            parts.append(_swa_band_part(blocks(sq), blocks(sk), blocks(sv), lw["sinks"], nc=nc))
            keep = min(WINDOW, T)
            new_k, new_v = sk3[:, T - keep:], sv3[:, T - keep:]
        else:
            kc = past[3][l].reshape(b, -1, SWA_KV_WIDTH)
            vc = past[4][l].reshape(b, -1, SWA_KV_WIDTH)
            parts.append(_swa_cache_part(sq.reshape(b, T, SWA_WIDTH), sk3, sv3, kc, vc, lw["sinks"], seqs=seqs))
            new_k = jnp.concatenate([kc[:, T:], sk3], axis=1)
            new_v = jnp.concatenate([vc[:, T:], sv3], axis=1)
        (ro, ret_all), (ys, ssd_all), (so,) = _run_parts(parts, (b // seqs, nc), "mixer")
        x1, h2, gates = _outproj(
            ro.reshape(n, RET_WIDTH), ys.reshape(n, D_INNER), so.reshape(n, SWA_WIDTH), x2d, mod[l],
            shared["norm2_g"][l], lw["w_out"], shared["wr_hi"], shared["wr_lo"], shared["router_bias"], tm=tm)
        x2d = _moe(h2, gates, x1, mod[l], lw["w_gate"], lw["w_up"], lw["w_down"], shared["final_g"],
                   tm=tm, final_norm=(l == DEPTH - 1))
        kv_shape = (b, -1, SWA_KV_HEADS, HEAD_DIM)
        convs.append(xbc.reshape(b, T, CONV_DIM)[:, T - (CONV_WIDTH - 1):])
        ks.append(new_k.reshape(kv_shape))
        vs.append(new_v.reshape(kv_shape))
    return x2d.reshape(b, T, D_MODEL), [ret_all, ssd_all, jnp.stack(convs), jnp.stack(ks), jnp.stack(vs)]


def kernel(x_prompt, x_sample, c_prompt, c_sample, state_ret, state_ssd, state_conv, cache_swa_k,
           cache_swa_v, ada_w, ada_b, norm1_g, norm2_g, w_in, conv_w, conv_b, dt_bias, A_log, D_skip,
           ssd_norm_g, sinks, w_out, router_w, router_bias, w_gate, w_up, w_down, final_g):
    bp, T, _ = x_prompt.shape
    bs, S, _ = x_sample.shape
    layers = [_layer_weights(l, w_in, dt_bias, A_log, D_skip, sinks, w_out, w_gate, w_up, w_down)
              for l in range(DEPTH)]
    wr_hi, wr_lo, rb = _router_layout(router_w, router_bias)
    shared = dict(norm1_g=norm1_g.reshape(DEPTH, 1, D_MODEL), norm2_g=norm2_g.reshape(DEPTH, 1, D_MODEL),
                  conv_w=conv_w, conv_b=conv_b.reshape(DEPTH, 1, CONV_DIM),
                  ssd_norm_g=ssd_norm_g.reshape(DEPTH, 1, D_INNER), wr_hi=wr_hi, wr_lo=wr_lo, router_bias=rb,
                  final_g=final_g.reshape(1, D_MODEL))
    mod = _modulation(jnp.concatenate([c_prompt, c_sample], axis=0), ada_w, ada_b)
    tm_s = bs * S
    mod_p = mod[:, :bp].reshape(DEPTH, bp, 1, 6 * D_MODEL)
    mod_s = jnp.repeat(mod[:, bp:], S, axis=1).reshape(DEPTH, 1, tm_s, 6 * D_MODEL)
    pos_p = jnp.arange(T, dtype=jnp.int32)
    pos_s = PAST_LEN + jnp.arange(S, dtype=jnp.int32)
    y_p, new_p = _trunk(x_prompt, mod_p, pos_p, None, layers, shared,
                        tm=TOKEN_TILE, chunk=CHUNK, seqs=1, act_dtype=BF16, use_bf16=True)
    y_s, new_s = _trunk(x_sample, mod_s, pos_s, (state_ret, state_ssd, state_conv, cache_swa_k, cache_swa_v),
                        layers, shared, tm=tm_s, chunk=S, seqs=SEQS_PER_STEP, act_dtype=F32, use_bf16=False)
    return (y_p, y_s, *new_p, *new_s)
```

```python
import functools

import jax
import jax.numpy as jnp
from jax import lax
from jax.experimental import pallas as pl
from jax.experimental.pallas import tpu as pltpu

F32 = jnp.float32
BF16 = jnp.bfloat16
HIGHEST = lax.Precision.HIGHEST

D_MODEL = 1024
DEPTH = 2
PAST_LEN = 16384
HEAD_DIM = 64
RET_HEADS = 4
RET_DK = 64
RET_DV = 64
RET_WIDTH = RET_HEADS * RET_DV
CHUNK = 128
ROPE_THETA = 10000.0
SSD_HEADS = 8
SSD_HEADDIM = 64
D_INNER = SSD_HEADS * SSD_HEADDIM
SSD_GROUPS = 2
SSD_DSTATE = 128
SSD_BC = SSD_GROUPS * SSD_DSTATE
CONV_WIDTH = 4
CONV_DIM = D_INNER + 2 * SSD_BC
SWA_HEADS = 4
SWA_KV_HEADS = 2
SWA_WIDTH = SWA_HEADS * HEAD_DIM
SWA_KV_WIDTH = SWA_KV_HEADS * HEAD_DIM
WINDOW = 128
N_EXPERTS = 16
N_EXPERT_GROUPS = 4
EXPERTS_PER_GROUP = N_EXPERTS // N_EXPERT_GROUPS
D_EXPERT = 256
NORM_EPS = 1e-6
LANES = 128
SUBLANES = 8
GATE_STRIDE = LANES // EXPERTS_PER_GROUP
VMEM_LIMIT = 56 * 1024 * 1024
TOKEN_TILE = 512
SEQS_PER_STEP = 8
SWA_HEAD_ORDER = (0, 2, 1, 3)

_OFF_XBC = 4 * RET_WIDTH + D_INNER
_OFF_DT = _OFF_XBC + CONV_DIM
_OFF_SQ = _OFF_DT + SSD_HEADS


def _silu(x):
    return x * (1.0 / (1.0 + jnp.exp(-x)))


def _dot(a, b, dims, use_bf16=True):
    if use_bf16:
        a = a.astype(BF16)
        b = b.astype(BF16)
    return lax.dot_general(a, b, (dims, ((), ())), preferred_element_type=F32)


_NN = ((1,), (0,))
_NT = ((1,), (1,))
_TN = ((0,), (0,))


def _split3(x, axis):
    t1 = x.astype(BF16)
    r1 = x - t1.astype(F32)
    t2 = r1.astype(BF16)
    t3 = (r1 - t2.astype(F32)).astype(BF16)
    return jnp.concatenate([t1, t2, t3], axis=axis)


def _sum3(y, axis):
    n = y.shape[axis] // 3
    parts = [lax.slice_in_dim(y, i * n, (i + 1) * n, axis=axis) for i in range(3)]
    return parts[0] + parts[1] + parts[2]


def _rows(parts):
    return parts[0] if len(parts) == 1 else jnp.concatenate(parts, axis=0)


def _params(sem):
    return pltpu.CompilerParams(dimension_semantics=sem, vmem_limit_bytes=VMEM_LIMIT)


def _const_spec(a):
    return pl.BlockSpec(a.shape, lambda *_: (0,) * a.ndim)


class _LayerState:
    def __init__(self, shape, layer, seqs, init, prev):
        self.shape, self.init, self.prev = shape, init, prev
        tail = (0,) * (len(shape) - 2)
        self.spec = pl.BlockSpec((None, seqs) + tuple(shape[2:]), lambda i, c: (layer, i) + tail)

    def add_inputs(self, operands, in_specs, aliases, out_index):
        if self.init is not None:
            operands.append(self.init)
            in_specs.append(self.spec)
        if self.prev is not None:
            aliases[len(operands)] = out_index
            operands.append(self.prev)
            in_specs.append(pl.BlockSpec(memory_space=pl.ANY))

    @property
    def out_shape(self):
        return jax.ShapeDtypeStruct(self.shape, F32)


class _Part:
    def __init__(self, body, operands, in_specs, out_specs, out_shape, scratch=(), aliases=None):
        self.body, self.operands, self.in_specs = body, operands, in_specs
        self.out_specs, self.out_shape, self.scratch = out_specs, out_shape, list(scratch)
        self.aliases = aliases or {}


def _run_parts(parts, grid, name):
    n_in = [len(p.operands) for p in parts]
    n_out = [len(p.out_shape) for p in parts]
    n_scr = [len(p.scratch) for p in parts]
    in_off = [sum(n_in[:i]) for i in range(len(parts))]
    out_off = [sum(n_out[:i]) for i in range(len(parts))]
    scr_off = [sum(n_scr[:i]) for i in range(len(parts))]
    aliases = {in_off[i] + a: out_off[i] + o for i, p in enumerate(parts) for a, o in p.aliases.items()}

    def kernel_fn(*refs):
        ins, outs = refs[:sum(n_in)], refs[sum(n_in):sum(n_in) + sum(n_out)]
        scr = refs[sum(n_in) + sum(n_out):]

        def run(phase):
            for i, p in enumerate(parts):
                p.body(ins[in_off[i]:in_off[i] + n_in[i]], outs[out_off[i]:out_off[i] + n_out[i]],
                       scr[scr_off[i]:scr_off[i] + n_scr[i]], phase)

        pl.when(pl.program_id(1) == 0)(lambda: run("init"))
        run("main")
        pl.when(pl.program_id(1) == pl.num_programs(1) - 1)(lambda: run("final"))

    res = pl.pallas_call(
        kernel_fn,
        grid=grid,
        in_specs=[s for p in parts for s in p.in_specs],
        out_specs=[s for p in parts for s in p.out_specs],
        out_shape=[s for p in parts for s in p.out_shape],
        scratch_shapes=[s for p in parts for s in p.scratch],
        input_output_aliases=aliases,
        compiler_params=_params(("arbitrary",) * len(grid)),
        name=name,
    )(*[o for p in parts for o in p.operands])
    return [res[out_off[i]:out_off[i] + n_out[i]] for i in range(len(parts))]


def _mod_kernel(c_ref, w_ref, b_ref, o_ref):
    c = c_ref[...]
    o_ref[...] = jnp.dot(_silu(c), w_ref[...], precision=HIGHEST,
                         preferred_element_type=F32) + b_ref[...]


def _modulation(c_all, ada_w, ada_b):
    rows = c_all.shape[0]
    tn = 512
    return pl.pallas_call(
        _mod_kernel,
        grid=(DEPTH, 6 * D_MODEL // tn),
        in_specs=[
            pl.BlockSpec((rows, D_MODEL), lambda l, j: (0, 0)),
            pl.BlockSpec((None, D_MODEL, tn), lambda l, j: (l, 0, j)),
            pl.BlockSpec((None, 1, tn), lambda l, j: (l, 0, j)),
        ],
        out_specs=pl.BlockSpec((None, rows, tn), lambda l, j: (l, 0, j)),
        out_shape=jax.ShapeDtypeStruct((DEPTH, rows, 6 * D_MODEL), F32),
        compiler_params=_params(("arbitrary", "arbitrary")),
        name="modulation",
    )(c_all, ada_w, ada_b.reshape(DEPTH, 1, 6 * D_MODEL))


def _rotate(p, cos, sin_signed):
    n = p.shape[-1]
    lane = lax.broadcasted_iota(jnp.int32, p.shape, 1)
    first_half = (lane % HEAD_DIM) < (HEAD_DIM // 2)
    partner = jnp.where(first_half, pltpu.roll(p, n - HEAD_DIM // 2, 1),
                        pltpu.roll(p, HEAD_DIM // 2, 1))
    return p * cos + partner * sin_signed


def _inproj_kernel(x_ref, g_ref, sc_ref, sh_ref, cos_ref, sin_ref, w_ref, ws_ref, wdt_ref,
                   ret_ref, z_ref, xbc_ref, dt_ref, sq_ref, sk_ref, sv_ref):
    x = x_ref[...]
    xn = x * lax.rsqrt(jnp.mean(x * x, axis=-1, keepdims=True) + NORM_EPS) * g_ref[...]
    h = (xn * (1.0 + sc_ref[...]) + sh_ref[...]).astype(BF16)
    cos = cos_ref[...]
    sin = sin_ref[...]

    def proj(ref, lo, width):
        return jnp.dot(h, ref[:, lo:lo + width], preferred_element_type=F32)

    w = RET_WIDTH
    ret_ref[:, 0:w] = _rotate(proj(w_ref, 0, w), cos, sin).astype(ret_ref.dtype)
    ret_ref[:, w:2 * w] = (_rotate(proj(w_ref, w, w), cos, sin) * RET_DK ** -0.5).astype(ret_ref.dtype)
    ret_ref[:, 2 * w:4 * w] = proj(w_ref, 2 * w, 2 * w).astype(ret_ref.dtype)
    z_ref[...] = proj(w_ref, 4 * w, D_INNER).astype(z_ref.dtype)
    xbc_ref[...] = proj(w_ref, _OFF_XBC, CONV_DIM)
    dt_ref[...] = jnp.dot(h, wdt_ref[...], preferred_element_type=F32)
    sq_ref[...] = (_rotate(proj(ws_ref, 0, SWA_WIDTH), cos, sin) * HEAD_DIM ** -0.5).astype(sq_ref.dtype)
    sk_ref[...] = _rotate(proj(ws_ref, SWA_WIDTH, SWA_KV_WIDTH), cos[:, :SWA_KV_WIDTH], sin[:, :SWA_KV_WIDTH])
    sv_ref[...] = proj(ws_ref, SWA_WIDTH + SWA_KV_WIDTH, SWA_KV_WIDTH)


def _inproj(x2d, gain, mod3, cos3, sin3, w_main, w_swa, w_dt, *, tm, act_dtype):
    n = x2d.shape[0]
    nt = n // tm
    per_mod = nt // mod3.shape[0]
    npos = cos3.shape[0]
    r = mod3.shape[1]
    row = lambda i: (i, 0)
    outs = [
        (4 * RET_WIDTH, act_dtype), (D_INNER, act_dtype), (CONV_DIM, F32), (LANES, F32),
        (SWA_WIDTH, act_dtype), (SWA_KV_WIDTH, F32), (SWA_KV_WIDTH, F32),
    ]
    return pl.pallas_call(
        _inproj_kernel,
        grid=(nt,),
        in_specs=[
            pl.BlockSpec((tm, D_MODEL), row),
            _const_spec(gain),
            pl.BlockSpec((None, r, D_MODEL), lambda i: (i // per_mod, 0, 1)),
            pl.BlockSpec((None, r, D_MODEL), lambda i: (i // per_mod, 0, 0)),
            pl.BlockSpec((None, tm, SWA_WIDTH), lambda i: (i % npos, 0, 0)),
            pl.BlockSpec((None, tm, SWA_WIDTH), lambda i: (i % npos, 0, 0)),
            _const_spec(w_main), _const_spec(w_swa), _const_spec(w_dt),
        ],
        out_specs=[pl.BlockSpec((tm, c), row) for c, _ in outs],
        out_shape=[jax.ShapeDtypeStruct((n, c), dt) for c, dt in outs],
        compiler_params=_params(("arbitrary",)),
        name="inproj",
    )(x2d, gain, mod3, mod3, cos3, sin3, w_main, w_swa, w_dt)


def _lane_masks(width, group, count, dtype):
    lane = lax.broadcasted_iota(jnp.int32, (1, width), 1) // group
    return [(lane == i).astype(F32).astype(dtype) for i in range(count)]


def _stack_masked(x, masks):
    return jnp.concatenate([x * m for m in masks], axis=0)


def _unstack_masked(y, masks):
    L = y.shape[0] // len(masks)
    out = y[0:L] * masks[0]
    for i in range(1, len(masks)):
        out = out + y[i * L:(i + 1) * L] * masks[i]
    return out


def _ret_tables(L):
    log_gamma = jnp.log(1.0 - 2.0 ** (-5.0 - jnp.arange(RET_HEADS, dtype=F32)))
    idx = jnp.arange(L, dtype=F32)
    diff = idx[:, None] - idx[None, :]
    decay = jnp.where(diff >= 0, jnp.exp(jnp.maximum(diff, 0.0)[None] * log_gamma[:, None, None]), 0.0)
    lane_head = jnp.arange(RET_WIDTH) // RET_DK
    lg = log_gamma[lane_head]
    same = lane_head[:, None] == lane_head[None, :]
    chunk_decay = jnp.exp(L * log_gamma)[lane_head]
    return [
        decay.reshape(RET_HEADS * L, L),
        jnp.exp((idx + 1.0)[:, None] * lg[None, :]),
        jnp.exp((L - 1.0 - idx)[:, None] * lg[None, :]),
        jnp.where(same, chunk_decay[:, None], 0.0),
        same.astype(F32),
    ]


def _ret_body(ins, outs, scratch, phase, *, has_init, use_bf16):
    blk_ref, decay_ref, qd_ref, kd_ref, sd_ref, bd_ref = ins[:6]
    s0_ref = ins[6] if has_init else None
    ro_ref, s_ref = outs
    sbd_ref, = scratch
    seqs, L = blk_ref.shape[0], blk_ref.shape[1]
    w, dk = RET_WIDTH, RET_DK

    if phase == "init":
        sbd_ref[...] = jnp.zeros(sbd_ref.shape, F32)
        if has_init:
            for j in range(seqs):
                for h in range(RET_HEADS):
                    sbd_ref[j, h * dk:(h + 1) * dk, h * dk:(h + 1) * dk] = s0_ref[j, h]
        return
    if phase == "final":
        for j in range(seqs):
            for h in range(RET_HEADS):
                s_ref[j, h] = sbd_ref[j, h * dk:(h + 1) * dk, h * dk:(h + 1) * dk]
        return

    block_diag = bd_ref[...]
    o_all, gate_all = [], []
    for j in range(seqs):
        blk = blk_ref[j]
        q, k, v = blk[:, 0:w], blk[:, w:2 * w], blk[:, 2 * w:3 * w]
        gate_all.append(blk[:, 3 * w:4 * w].astype(F32))
        masks = _lane_masks(w, dk, RET_HEADS, q.dtype)
        att = _dot(_stack_masked(q, masks), k, _NT, use_bf16) * decay_ref[...]
        o = _unstack_masked(_dot(att, v, _NN, use_bf16), _lane_masks(w, dk, RET_HEADS, F32))
        s = sbd_ref[j]
        o_all.append(o + _dot(q, s, _NN, use_bf16) * qd_ref[...])
        sbd_ref[j] = s * sd_ref[...] + _dot(k.astype(F32) * kd_ref[...], v, _TN, use_bf16) * block_diag
    o = _rows(o_all)
    mean = _dot(o, block_diag, _NN, use_bf16) * (1.0 / RET_DV)
    cen = o - mean
    var = _dot(cen * cen, block_diag, _NN, use_bf16) * (1.0 / RET_DV)
    out = _silu(_rows(gate_all)) * (cen * lax.rsqrt(var + NORM_EPS))
    for j in range(seqs):
        ro_ref[j] = out[j * L:(j + 1) * L].astype(ro_ref.dtype)


def _retention_part(ret3, state, *, nc, seqs, out_dtype, use_bf16):
    nblk, L, _ = ret3.shape
    tables = _ret_tables(L)
    blk = lambda c: pl.BlockSpec((seqs, L, c), lambda i, j: (i * nc + j, 0, 0))
    operands = [ret3] + tables
    in_specs = [blk(4 * RET_WIDTH)] + [_const_spec(a) for a in tables]
    aliases = {}
    state.add_inputs(operands, in_specs, aliases, 1)
    return _Part(
        functools.partial(_ret_body, has_init=state.init is not None, use_bf16=use_bf16),
        operands, in_specs, [blk(RET_WIDTH), state.spec],
        [jax.ShapeDtypeStruct((nblk, L, RET_WIDTH), out_dtype), state.out_shape],
        [pltpu.VMEM((seqs, RET_WIDTH, RET_WIDTH), F32)], aliases)


def _softplus(x):
    return jnp.maximum(x, 0.0) + jnp.log1p(jnp.exp(-jnp.abs(x)))


def _ssd_expand_table():
    src = jnp.arange(LANES)
    dst = jnp.arange(2 * SSD_HEADS * LANES)
    expand = (dst[None, :] // LANES == src[:, None]) & (src[:, None] < 2 * SSD_HEADS)
    return jnp.tile(expand.astype(BF16), (3, 1))


def _per_head64(e):
    lane = lax.broadcasted_iota(jnp.int32, (1, LANES), 1)
    low = lane < SSD_HEADDIM
    cols = [jnp.where(low, e[:, (2 * c) * LANES:(2 * c + 1) * LANES], e[:, (2 * c + 1) * LANES:(2 * c + 2) * LANES])
            for c in range(SSD_HEADS // 2)]
    return jnp.concatenate(cols, axis=-1)


def _ssd_body(ins, outs, scratch, phase, *, has_init, use_bf16):
    xbc_ref, z_ref, dt_ref, cw_ref, cb_ref, hp_ref, dskip_ref, ng_ref, expand_ref = ins[:9]
    tail0_ref, h0_ref = (ins[9], ins[10]) if has_init else (None, None)
    ys_ref, h_ref = outs
    tail_ref, hbd_ref = scratch
    seqs, L = xbc_ref.shape[0], xbc_ref.shape[1]
    R = seqs * L
    keep = CONV_WIDTH - 1
    n, p = SSD_DSTATE, SSD_HEADDIM
    per_group = SSD_HEADS // SSD_GROUPS

    if phase == "init":
        hbd_ref[...] = jnp.zeros(hbd_ref.shape, F32)
        tail_ref[...] = tail0_ref[...] if has_init else jnp.zeros(tail_ref.shape, F32)
        if has_init:
            for j in range(seqs):
                for h in range(SSD_HEADS):
                    g = h // per_group
                    hbd_ref[j, h * p:(h + 1) * p, g * n:(g + 1) * n] = h0_ref[j, h]
        return
    if phase == "final":
        for j in range(seqs):
            for h in range(SSD_HEADS):
                g = h // per_group
                h_ref[j, h] = hbd_ref[j, h * p:(h + 1) * p, g * n:(g + 1) * n]
        return

    cw = cw_ref[...]
    hp = hp_ref[...]
    lane = lax.broadcasted_iota(jnp.int32, (1, LANES), 1)
    gmask = _lane_masks(SSD_BC, n, SSD_GROUPS, F32)
    hmask = _lane_masks(per_group * p, p, per_group, F32)

    xs_l, bs_l, cs_l = [], [], []
    for j in range(seqs):
        cur = xbc_ref[j]
        ext = jnp.concatenate([tail_ref[j], cur], axis=0)
        xc = cw[keep:keep + 1] * cur
        for wi in range(keep):
            lo = SUBLANES - keep + wi
            xc = xc + cw[wi:wi + 1] * ext[lo:lo + L]
        if L >= SUBLANES:
            tail_ref[j] = cur[L - SUBLANES:]
        xc = _silu(xc + cb_ref[...])
        xs_l.append(xc[:, :D_INNER])
        bs_l.append(xc[:, D_INNER:D_INNER + SSD_BC])
        cs_l.append(xc[:, D_INNER + SSD_BC:])

    dtv = _softplus(_rows([dt_ref[j] for j in range(seqs)]) + hp[0:1])
    a = dtv * (-jnp.exp(hp[1:2]))
    row = lax.broadcasted_iota(jnp.int32, (R, R), 0)
    col = lax.broadcasted_iota(jnp.int32, (R, R), 1)
    causal = (row >= col) & (row // L == col // L)
    acum = _sum3(jnp.dot(causal.astype(F32).astype(BF16), _split3(a, 1), preferred_element_type=F32), 1)
    pick = (lax.broadcasted_iota(jnp.int32, (SUBLANES, LANES), 0)
            == lax.broadcasted_iota(jnp.int32, (SUBLANES, LANES), 1)).astype(F32).astype(BF16)
    acum_t = _sum3(lax.dot_general(pick, _split3(acum, 0), (_NT, ((), ())),
                                   preferred_element_type=F32), 1)
    both = jnp.where(lane < SSD_HEADS, dtv, pltpu.roll(acum, SSD_HEADS, 1))
    spread = jnp.dot(_split3(both, 1), expand_ref[...], preferred_element_type=F32)
    acum_e = spread[:, SSD_HEADS * LANES:]
    dt64 = _per_head64(spread[:, :SSD_HEADS * LANES])
    acum64 = _per_head64(acum_e)
    tri = (lax.broadcasted_iota(jnp.int32, (L, L), 0) >= lax.broadcasted_iota(jnp.int32, (L, L), 1))

    y_l = []
    for j in range(seqs):
        r0 = j * L
        xs, bs, cs = xs_l[j], bs_l[j], cs_l[j]
        a64, d64 = acum64[r0:r0 + L], dt64[r0:r0 + L]
        cb = _dot(_stack_masked(cs, gmask), bs, _NT, use_bf16)
        xdt = xs * d64
        y_parts = []
        for g in range(SSD_GROUPS):
            atts = []
            for h in range(g * per_group, (g + 1) * per_group):
                a_col = acum_e[r0:r0 + L, h * LANES:h * LANES + L]
                seg = jnp.exp(jnp.where(tri, a_col - acum_t[h:h + 1, r0:r0 + L], -jnp.inf))
                atts.append(cb[g * L:(g + 1) * L] * seg)
            yg = _dot(jnp.concatenate(atts, axis=0), xdt[:, g * per_group * p:(g + 1) * per_group * p],
                      _NN, use_bf16)
            y_parts.append(_unstack_masked(yg, hmask))
        state = hbd_ref[j]
        y_l.append(jnp.concatenate(y_parts, axis=-1) + _dot(cs, state, _NT, use_bf16) * jnp.exp(a64)
                   + dskip_ref[...] * xs)
        w_end = jnp.exp(a64[L - 1:L] - a64) * d64
        kv = _dot(xs * w_end, bs, _TN, use_bf16)
        for h in range(SSD_HEADS):
            g = h // per_group
            carry = jnp.exp(acum_t[h:h + 1, r0 + L - 1:r0 + L])
            hbd_ref[j, h * p:(h + 1) * p, g * n:(g + 1) * n] = (
                carry * state[h * p:(h + 1) * p, g * n:(g + 1) * n] + kv[h * p:(h + 1) * p, g * n:(g + 1) * n])

    ys = _rows(y_l) * _silu(_rows([z_ref[j] for j in range(seqs)]).astype(F32))
    ys = ys * lax.rsqrt(jnp.mean(ys * ys, axis=-1, keepdims=True) + NORM_EPS) * ng_ref[...]
    for j in range(seqs):
        ys_ref[j] = ys[j * L:(j + 1) * L].astype(ys_ref.dtype)


def _ssd_part(xbc3, z3, dt3, tail0, state, conv_w, conv_b, head_params, dskip64, norm_g, *, nc, seqs, out_dtype,
              use_bf16):
    nblk, L, _ = xbc3.shape
    assert L >= SUBLANES or nc == 1, "the conv tail is only carried between chunks of at least 8 rows"
    blk = lambda c: pl.BlockSpec((seqs, L, c), lambda i, j: (i * nc + j, 0, 0))
    consts = [conv_w, conv_b, head_params, dskip64, norm_g, _ssd_expand_table()]
    operands = [xbc3, z3, dt3] + consts
    in_specs = [blk(CONV_DIM), blk(D_INNER), blk(LANES)] + [_const_spec(a) for a in consts]
    aliases = {}
    if state.init is not None:
        operands.append(tail0)
        in_specs.append(pl.BlockSpec((seqs, SUBLANES, CONV_DIM), lambda i, j: (i, 0, 0)))
    state.add_inputs(operands, in_specs, aliases, 1)
    return _Part(
        functools.partial(_ssd_body, has_init=state.init is not None, use_bf16=use_bf16),
        operands, in_specs, [blk(D_INNER), state.spec],
        [jax.ShapeDtypeStruct((nblk, L, D_INNER), out_dtype), state.out_shape],
        [pltpu.VMEM((seqs, SUBLANES, CONV_DIM), F32), pltpu.VMEM((seqs, D_INNER, SSD_BC), F32)], aliases)


def _swa_stack_queries(q):
    lo, hi = _lane_masks(SWA_KV_WIDTH, HEAD_DIM, SWA_KV_HEADS, q.dtype)
    c0, c1 = q[:, :SWA_KV_WIDTH], q[:, SWA_KV_WIDTH:]
    return jnp.concatenate([c0 * lo, c1 * lo, c0 * hi, c1 * hi], axis=0)


def _swa_unstack(o):
    L = o.shape[0] // SWA_HEADS
    lo, hi = _lane_masks(SWA_KV_WIDTH, HEAD_DIM, SWA_KV_HEADS, F32)
    return jnp.concatenate([o[0:L] * lo + o[2 * L:3 * L] * hi, o[L:2 * L] * lo + o[3 * L:4 * L] * hi], axis=-1)


def _sink_column(sinks, L):
    return jnp.concatenate([jnp.broadcast_to(sinks[:, h:h + 1], (L, 1)) for h in range(SWA_HEADS)], axis=0)


def _sink_softmax_pv(parts, sink):
    m = sink
    for s, _ in parts:
        m = jnp.maximum(m, jnp.max(s, axis=-1, keepdims=True))
    acc = None
    for s, v in parts:
        v_ext = jnp.concatenate([v.astype(BF16), jnp.ones(v.shape, BF16)], axis=-1)
        pv = _dot(jnp.exp(s - m), v_ext, _NN)
        acc = pv if acc is None else acc + pv
    width = parts[0][1].shape[-1]
    return acc[:, :width] / (acc[:, width:] + jnp.exp(sink - m))


def _swa_band_body(ins, outs, scratch, phase):
    if phase != "main":
        return
    q_ref, kp_ref, kc_ref, vp_ref, vc_ref, sink_ref = ins
    o_ref, = outs
    L = q_ref.shape[1]
    c = pl.program_id(1)
    keys = jnp.concatenate([kp_ref[0], kc_ref[0]], axis=0).astype(BF16)
    vals = jnp.concatenate([vp_ref[0], vc_ref[0]], axis=0)
    row = lax.broadcasted_iota(jnp.int32, (SWA_HEADS * L, 2 * L), 0) % L
    col = lax.broadcasted_iota(jnp.int32, (SWA_HEADS * L, 2 * L), 1)
    rel = row - col + L
    valid = (rel >= 0) & (rel < WINDOW) & ((col >= L) | (c > 0))
    s = jnp.where(valid, _dot(_swa_stack_queries(q_ref[0]), keys, _NT), -1e30)
    o = _sink_softmax_pv([(s, vals)], _sink_column(sink_ref[...], L))
    o_ref[0] = _swa_unstack(o).astype(o_ref.dtype)


def _swa_band_part(sq3, sk3, sv3, sinks_row, *, nc):
    nblk, L, _ = sq3.shape
    cur = lambda c: pl.BlockSpec((1, L, c), lambda i, j: (i * nc + j, 0, 0))
    prev = lambda c: pl.BlockSpec((1, L, c), lambda i, j: (i * nc + jnp.maximum(j - 1, 0), 0, 0))
    return _Part(
        _swa_band_body, [sq3, sk3, sk3, sv3, sv3, sinks_row],
        [cur(SWA_WIDTH), prev(SWA_KV_WIDTH), cur(SWA_KV_WIDTH), prev(SWA_KV_WIDTH), cur(SWA_KV_WIDTH),
         _const_spec(sinks_row)],
        [cur(SWA_WIDTH)], [jax.ShapeDtypeStruct((nblk, L, SWA_WIDTH), BF16)])


def _swa_cache_body(ins, outs, scratch, phase):
    if phase != "main":
        return
    q_ref, kn_ref, vn_ref, kc_ref, vc_ref, sink_ref = ins
    o_ref, = outs
    seqs, T = q_ref.shape[0], q_ref.shape[1]
    W = kc_ref.shape[1]
    rows = SWA_HEADS * T
    qt = lax.broadcasted_iota(jnp.int32, (rows, W), 0) % T
    kcol = lax.broadcasted_iota(jnp.int32, (rows, W), 1)
    rel_c = qt + W - kcol
    valid_c = (rel_c >= 0) & (rel_c < WINDOW) & (kcol + (PAST_LEN - W) >= 0)
    rel_n = (lax.broadcasted_iota(jnp.int32, (rows, T), 0) % T) - lax.broadcasted_iota(jnp.int32, (rows, T), 1)
    valid_n = (rel_n >= 0) & (rel_n < WINDOW)
    sink = _sink_column(sink_ref[...], T)
    for j in range(seqs):
        qs = _swa_stack_queries(q_ref[j])
        s_c = jnp.where(valid_c, _dot(qs, kc_ref[j], _NT), -1e30)
        s_n = jnp.where(valid_n, _dot(qs, kn_ref[j], _NT), -1e30)
        o = _sink_softmax_pv([(s_c, vc_ref[j]), (s_n, vn_ref[j])], sink)
        o_ref[j] = _swa_unstack(o).astype(o_ref.dtype)


def _swa_cache_part(sq3, sk3, sv3, k_cache, v_cache, sinks_row, *, seqs):
    b, T, _ = sq3.shape
    W = k_cache.shape[1]
    new = lambda c: pl.BlockSpec((seqs, T, c), lambda i, j: (i, 0, 0))
    cache = pl.BlockSpec((seqs, W, SWA_KV_WIDTH), lambda i, j: (i, 0, 0))
    return _Part(
        _swa_cache_body, [sq3, sk3, sv3, k_cache, v_cache, sinks_row],
        [new(SWA_WIDTH), new(SWA_KV_WIDTH), new(SWA_KV_WIDTH), cache, cache, _const_spec(sinks_row)],
        [new(SWA_WIDTH)], [jax.ShapeDtypeStruct((b, T, SWA_WIDTH), F32)])


def _router_gates(hn, h_hi, wr_hi_ref, wr_lo_ref, rb_ref):
    h_lo = (hn - h_hi.astype(F32)).astype(BF16)
    logits = (jnp.dot(h_hi, wr_hi_ref[...], preferred_element_type=F32)
              + jnp.dot(h_lo, wr_hi_ref[...], preferred_element_type=F32)
              + jnp.dot(h_hi, wr_lo_ref[...], preferred_element_type=F32))
    scores = 1.0 / (1.0 + jnp.exp(-logits))
    biased = scores + rb_ref[...]
    lane = lax.broadcasted_iota(jnp.int32, logits.shape, 1)
    group_lane = lane < N_EXPERT_GROUPS
    sk = [scores] + [pltpu.roll(scores, LANES - GATE_STRIDE * k, 1) for k in range(1, EXPERTS_PER_GROUP)]
    bk = [biased] + [pltpu.roll(biased, LANES - GATE_STRIDE * k, 1) for k in range(1, EXPERTS_PER_GROUP)]
    hi1, lo1 = jnp.maximum(bk[0], bk[1]), jnp.minimum(bk[0], bk[1])
    hi2, lo2 = jnp.maximum(bk[2], bk[3]), jnp.minimum(bk[2], bk[3])
    top1 = jnp.maximum(hi1, hi2)
    top2 = jnp.maximum(jnp.minimum(hi1, hi2), jnp.maximum(lo1, lo2))
    group_score = jnp.where(group_lane, top1 + top2, -jnp.inf)
    best_score = jnp.max(group_score, axis=-1, keepdims=True)
    best = jnp.min(jnp.where(group_score == best_score, lane, LANES), axis=-1, keepdims=True)
    in_group = lane == best
    picked = []
    for k in range(EXPERTS_PER_GROUP):
        rank = jnp.zeros(logits.shape, jnp.int32)
        for j in range(EXPERTS_PER_GROUP):
            if j == k:
                continue
            ahead = (bk[j] >= bk[k]) if j < k else (bk[j] > bk[k])
            rank = rank + ahead.astype(jnp.int32)
        picked.append(jnp.where((rank < 2) & in_group, sk[k], 0.0))
    total = jnp.sum(picked[0] + picked[1] + picked[2] + picked[3], axis=-1, keepdims=True)
    gates = picked[0] / total
    for k in range(1, EXPERTS_PER_GROUP):
        gates = gates + pltpu.roll(picked[k] / total, GATE_STRIDE * k, 1)
    return gates


def _outmoe_kernel(ro_ref, ys_ref, so_ref, x_ref, g1_ref, sc_ref, sh_ref, g2_ref, n2_ref, w_ref, wr_hi_ref,
                   wr_lo_ref, rb_ref, wg_ref, wu_ref, wd_ref, fg_ref, o_ref, *, final_norm, halves):
    tm = x_ref.shape[0]
    hm = tm // halves
    per_row = g1_ref.shape[0] == tm

    def mod(ref, rows):
        return ref[rows, :] if per_row else ref[...]

    for s in range(halves):
        rows = slice(s * hm, (s + 1) * hm)
        mix = jnp.concatenate([ro_ref[rows, :].astype(BF16), ys_ref[rows, :].astype(BF16),
                               so_ref[rows, :].astype(BF16)], axis=-1)
        x1 = x_ref[rows, :] + mod(g1_ref, rows) * jnp.dot(mix, w_ref[...], preferred_element_type=F32)
        xn = x1 * lax.rsqrt(jnp.mean(x1 * x1, axis=-1, keepdims=True) + NORM_EPS) * n2_ref[...]
        hn = xn * (1.0 + mod(sc_ref, rows)) + mod(sh_ref, rows)
        h = hn.astype(BF16)
        gates = _router_gates(hn, h, wr_hi_ref, wr_lo_ref, rb_ref)
        acts = []
        for e in range(N_EXPERTS):
            lane = GATE_STRIDE * (e % EXPERTS_PER_GROUP) + e // EXPERTS_PER_GROUP
            g = jnp.dot(h, wg_ref[e], preferred_element_type=F32)
            u = jnp.dot(h, wu_ref[e], preferred_element_type=F32)
            acts.append((_silu(g) * u * gates[:, lane:lane + 1]).astype(BF16))
        mixed = jnp.dot(jnp.concatenate(acts, axis=-1), wd_ref[...], preferred_element_type=F32)
        x2 = x1 + mod(g2_ref, rows) * mixed
        if final_norm:
            x2 = x2 * lax.rsqrt(jnp.mean(x2 * x2, axis=-1, keepdims=True) + NORM_EPS) * fg_ref[...]
        o_ref[rows, :] = x2


def _outmoe(ro, ys, so, x2d, mod3, gain2, w_out, wr_hi, wr_lo, router_bias, w_gate, w_up, w_down, final_g, *,
            tm, final_norm):
    n = x2d.shape[0]
    nt = n // tm
    per_mod = nt // mod3.shape[0]
    r = mod3.shape[1]
    row = lambda c: pl.BlockSpec((tm, c), lambda i: (i, 0))
    mod = lambda j: pl.BlockSpec((None, r, D_MODEL), lambda i: (i // per_mod, 0, j))
    consts = [gain2, w_out, wr_hi, wr_lo, router_bias, w_gate, w_up, w_down, final_g]
    return pl.pallas_call(
        functools.partial(_outmoe_kernel, final_norm=final_norm, halves=2),
        grid=(nt,),
        in_specs=[row(RET_WIDTH), row(D_INNER), row(SWA_WIDTH), row(D_MODEL),
                  mod(2), mod(4), mod(3), mod(5)] + [_const_spec(a) for a in consts],
        out_specs=row(D_MODEL),
        out_shape=jax.ShapeDtypeStruct((n, D_MODEL), F32),
        compiler_params=_params(("arbitrary",)),
        name="outmoe",
    )(ro, ys, so, x2d, mod3, mod3, mod3, mod3, *consts)


def _rope_tables(pos):
    half = HEAD_DIM // 2
    inv = ROPE_THETA ** (-jnp.arange(half, dtype=F32) / half)
    ang = pos.astype(F32)[:, None] * inv[None, :]
    cos = jnp.cos(ang)
    sin = jnp.sin(ang)
    cos = jnp.tile(jnp.concatenate([cos, cos], axis=-1), (1, SWA_WIDTH // HEAD_DIM))
    sin = jnp.tile(jnp.concatenate([-sin, sin], axis=-1), (1, SWA_WIDTH // HEAD_DIM))
    return cos, sin


def _pad_lanes(v):
    return jnp.pad(v, (0, LANES - v.shape[0])).reshape(1, LANES)


def _swa_head_permute(w, axis):
    blocks = [lax.slice_in_dim(w, h * HEAD_DIM, (h + 1) * HEAD_DIM, axis=axis) for h in SWA_HEAD_ORDER]
    return jnp.concatenate(blocks, axis=axis)


def _layer_weights(l, w_in, dt_bias, A_log, D_skip, sinks, w_out, w_gate, w_up, w_down):
    wl = w_in[l]
    w_dt = jnp.pad(wl[:, _OFF_DT:_OFF_SQ], ((0, 0), (0, LANES - SSD_HEADS))).astype(BF16)
    head_params = jnp.concatenate(
        [_pad_lanes(dt_bias[l]), _pad_lanes(A_log[l]), jnp.zeros((SUBLANES - 2, LANES), F32)], axis=0)
    w_swa = jnp.concatenate([_swa_head_permute(wl[:, _OFF_SQ:_OFF_SQ + SWA_WIDTH], 1),
                             wl[:, _OFF_SQ + SWA_WIDTH:]], axis=1)
    wo = w_out[l]
    w_o = jnp.concatenate([wo[:RET_WIDTH + D_INNER], _swa_head_permute(wo[RET_WIDTH + D_INNER:], 0)], axis=0)
    return dict(w_main=wl[:, :_OFF_DT].astype(BF16), w_swa=w_swa.astype(BF16), w_dt=w_dt,
                head_params=head_params, dskip64=jnp.repeat(D_skip[l].astype(F32), SSD_HEADDIM).reshape(1, D_INNER),
                sinks=_pad_lanes(sinks[l]),
                w_out=w_o.astype(BF16), w_gate=w_gate[l].astype(BF16), w_up=w_up[l].astype(BF16),
                w_down=w_down[l].astype(BF16).reshape(N_EXPERTS * D_EXPERT, D_MODEL))


def _router_layout(router_w, router_bias):
    e = jnp.arange(N_EXPERTS)
    lanes = GATE_STRIDE * (e % EXPERTS_PER_GROUP) + e // EXPERTS_PER_GROUP
    w = jnp.zeros((D_MODEL, LANES), F32).at[:, lanes].set(router_w.astype(F32))
    b = jnp.zeros((1, LANES), F32).at[0, lanes].set(router_bias.astype(F32))
    w_hi = w.astype(BF16)
    w_lo = (w - w_hi.astype(F32)).astype(BF16)
    return w_hi, w_lo, b


def _trunk(x, mod, pos, past, layers, shared, *, tm, chunk, seqs, act_dtype, use_bf16):
    b, T, _ = x.shape
    n = b * T
    nc = T // chunk
    cos, sin = _rope_tables(pos)
    reps = tm // T if tm > T else 1
    cos3 = jnp.tile(cos, (reps, 1)).reshape(-1, tm, SWA_WIDTH)
    sin3 = jnp.tile(sin, (reps, 1)).reshape(-1, tm, SWA_WIDTH)
    x2d = x.reshape(n, D_MODEL)
    ret_all = ssd_all = None
    convs, ks, vs = [], [], []
    for l in range(DEPTH):
        lw = layers[l]
        ret, z, xbc, dt, sq, sk, sv = _inproj(
            x2d, shared["norm1_g"][l], mod[l], cos3, sin3, lw["w_main"], lw["w_swa"], lw["w_dt"],
            tm=tm, act_dtype=act_dtype)
        blocks = lambda a: a.reshape(b * nc, chunk, a.shape[-1])
        ret_state = _LayerState((DEPTH, b, RET_HEADS, RET_DK, RET_DV), l, seqs,
                                None if past is None else past[0], ret_all)
        ssd_state = _LayerState((DEPTH, b, SSD_HEADS, SSD_HEADDIM, SSD_DSTATE), l, seqs,
                                None if past is None else past[1], ssd_all)
        tail0 = None if past is None else jnp.pad(
            past[2][l], ((0, 0), (SUBLANES - (CONV_WIDTH - 1), 0), (0, 0)))
        parts = [
            _retention_part(blocks(ret), ret_state, nc=nc, seqs=seqs, out_dtype=act_dtype, use_bf16=use_bf16),
            _ssd_part(blocks(xbc), blocks(z), blocks(dt), tail0, ssd_state, shared["conv_w"][l],
                      shared["conv_b"][l], lw["head_params"], lw["dskip64"], shared["ssd_norm_g"][l],
                      nc=nc, seqs=seqs, out_dtype=act_dtype, use_bf16=use_bf16),
        ]
        sk3 = sk.reshape(b, T, SWA_KV_WIDTH)
        sv3 = sv.reshape(b, T, SWA_KV_WIDTH)
        if past is None:
            parts.append(_swa_band_part(blocks(sq), blocks(sk), blocks(sv), lw["sinks"], nc=nc))
            keep = min(WINDOW, T)
            new_k, new_v = sk3[:, T - keep:], sv3[:, T - keep:]
        else:
            kc = past[3][l].reshape(b, -1, SWA_KV_WIDTH)
            vc = past[4][l].reshape(b, -1, SWA_KV_WIDTH)
            parts.append(_swa_cache_part(sq.reshape(b, T, SWA_WIDTH), sk3, sv3, kc, vc, lw["sinks"], seqs=seqs))
            new_k = jnp.concatenate([kc[:, T:], sk3], axis=1)
            new_v = jnp.concatenate([vc[:, T:], sv3], axis=1)
        (ro, ret_all), (ys, ssd_all), (so,) = _run_parts(parts, (b // seqs, nc), "mixer")
        x2d = _outmoe(
            ro.reshape(n, RET_WIDTH), ys.reshape(n, D_INNER), so.reshape(n, SWA_WIDTH), x2d, mod[l],
            shared["norm2_g"][l], lw["w_out"], shared["wr_hi"], shared["wr_lo"], shared["router_bias"],
            lw["w_gate"], lw["w_up"], lw["w_down"], shared["final_g"], tm=tm, final_norm=(l == DEPTH - 1))
        kv_shape = (b, -1, SWA_KV_HEADS, HEAD_DIM)
        convs.append(xbc.reshape(b, T, CONV_DIM)[:, T - (CONV_WIDTH - 1):])
        ks.append(new_k.reshape(kv_shape))
        vs.append(new_v.reshape(kv_shape))
    return x2d.reshape(b, T, D_MODEL), [ret_all, ssd_all, jnp.stack(convs), jnp.stack(ks), jnp.stack(vs)]


def kernel(x_prompt, x_sample, c_prompt, c_sample, state_ret, state_ssd, state_conv, cache_swa_k,
           cache_swa_v, ada_w, ada_b, norm1_g, norm2_g, w_in, conv_w, conv_b, dt_bias, A_log, D_skip,
           ssd_norm_g, sinks, w_out, router_w, router_bias, w_gate, w_up, w_down, final_g):
    bp, T, _ = x_prompt.shape
    bs, S, _ = x_sample.shape
    layers = [_layer_weights(l, w_in, dt_bias, A_log, D_skip, sinks, w_out, w_gate, w_up, w_down)
              for l in range(DEPTH)]
    wr_hi, wr_lo, rb = _router_layout(router_w, router_bias)
    shared = dict(norm1_g=norm1_g.reshape(DEPTH, 1, D_MODEL), norm2_g=norm2_g.reshape(DEPTH, 1, D_MODEL),
                  conv_w=conv_w, conv_b=conv_b.reshape(DEPTH, 1, CONV_DIM),
                  ssd_norm_g=ssd_norm_g.reshape(DEPTH, 1, D_INNER), wr_hi=wr_hi, wr_lo=wr_lo, router_bias=rb,
                  final_g=final_g.reshape(1, D_MODEL))
    mod = _modulation(jnp.concatenate([c_prompt, c_sample], axis=0), ada_w, ada_b)
    tm_s = bs * S
    mod_p = mod[:, :bp].reshape(DEPTH, bp, 1, 6 * D_MODEL)
    mod_s = jnp.repeat(mod[:, bp:], S, axis=1).reshape(DEPTH, 1, tm_s, 6 * D_MODEL)
    pos_p = jnp.arange(T, dtype=jnp.int32)
    pos_s = PAST_LEN + jnp.arange(S, dtype=jnp.int32)
    y_p, new_p = _trunk(x_prompt, mod_p, pos_p, None, layers, shared,
                        tm=TOKEN_TILE, chunk=CHUNK, seqs=1, act_dtype=BF16, use_bf16=True)
    y_s, new_s = _trunk(x_sample, mod_s, pos_s, (state_ret, state_ssd, state_conv, cache_swa_k, cache_swa_v),
                        layers, shared, tm=tm_s, chunk=S, seqs=SEQS_PER_STEP, act_dtype=F32, use_bf16=False)
    return (y_p, y_s, *new_p, *new_s)
```
